```python
import jax, jax.numpy as jnp
from jax import lax
import numpy as np

D_MODEL = 1024
BATCH = 8
SEQ = 2048
DEPTH = 4

N_MIXERS = 3
DEEPNORM_ALPHA = (2.0 * DEPTH) ** 0.25
DEEPNORM_BETA = (8.0 * DEPTH) ** -0.25
LN_EPS = 1e-5
NEG_INF = -1e30

CONV_KERNEL = 31
SHORT_KERNEL = 3
NSA_HEADS = 16
NSA_KV_HEADS = 4
NSA_GROUP = NSA_HEADS // NSA_KV_HEADS
HEAD_DIM = D_MODEL // NSA_HEADS
CMP_BLOCK = 32
CMP_STRIDE = 16
SLC_BLOCK = 64
SLC_TOP_N = 16
SLC_FORCED_LOCAL = 2
FORCE_BONUS = 1e4
WINDOW = 512
PHI_HIDDEN = 2 * HEAD_DIM
NSA_QBLK = 64
NSA_IN_COLS = NSA_HEADS * HEAD_DIM + 6 * NSA_KV_HEADS * HEAD_DIM + 3 * NSA_HEADS
D_FF = 2816
N_EXPERTS = 8
TOP_K = 2
D_FF_EXPERT = 3584

kernel_name = "hybrid_conformer_shortconv_nsa_moe_deepnorm"


def _count(period, phase):
    return len(range(phase, DEPTH, period))


def _layer_norm(x, g, b):
    xf = x.astype(jnp.float32)
    mu = jnp.mean(xf, -1, keepdims=True)
    var = jnp.mean(jnp.square(xf - mu), -1, keepdims=True)
    y = (xf - mu) * lax.rsqrt(var + LN_EPS)
    return (y * g.astype(jnp.float32) + b.astype(jnp.float32)).astype(x.dtype)


def _causal_depthwise_conv(x, w):
    k = w.shape[0]
    return lax.conv_general_dilated(
        x, w[:, None, :], window_strides=(1,), padding=[(k - 1, 0)],
        dimension_numbers=("NWC", "WIO", "NWC"), feature_group_count=x.shape[-1])


def _masked_softmax(s, mask):
    s = jnp.where(mask, s.astype(jnp.float32), NEG_INF)
    m = jnp.max(s, -1, keepdims=True)
    e = jnp.exp(s - m) * mask
    return e / jnp.maximum(jnp.sum(e, -1, keepdims=True), 1e-30)


def conformer_conv(x, w_in, w_dw, b_dw, ln_g, ln_b, w_out):
    a, gate = jnp.split(x @ w_in, 2, axis=-1)
    u = a * jax.nn.sigmoid(gate)
    u = _causal_depthwise_conv(u, w_dw) + b_dw
    u = jax.nn.silu(_layer_norm(u, ln_g, ln_b))
    return u @ w_out


def short_gated_conv(x, w_in, w_conv, w_out):
    b_gate, c_gate, h = jnp.split(x @ w_in, 3, axis=-1)
    y = b_gate * _causal_depthwise_conv(c_gate * h, w_conv)
    return y @ w_out


def _compress(kv, pe, w1, w2):
    B, T, Hk, dh = kv.shape
    n_cmp = (T - CMP_BLOCK) // CMP_STRIDE + 1
    idx = jnp.arange(n_cmp)[:, None] * CMP_STRIDE + jnp.arange(CMP_BLOCK)[None, :]
    blocks = kv[:, idx] + pe[:, None, :]
    blocks = jnp.moveaxis(blocks, 3, 1).reshape(B, Hk, n_cmp, CMP_BLOCK * dh)
    return jax.nn.gelu(blocks @ w1) @ w2


def native_sparse_attention(x, w_in, pe_k, pe_v, wk1, wk2, wv1, wv2, w_out):
    B, T, _ = x.shape
    H, Hk, G, dh = NSA_HEADS, NSA_KV_HEADS, NSA_GROUP, HEAD_DIM
    qd, kd = H * dh, Hk * dh
    cuts = [qd, qd + kd, qd + 2 * kd, qd + 3 * kd, qd + 4 * kd, qd + 5 * kd, qd + 6 * kd]
    q, kc, vc, ks, vs, kw, vw, g = jnp.split(x @ w_in, cuts, axis=-1)
    q = q.reshape(B, T, Hk, G, dh).transpose(0, 2, 3, 1, 4) * (dh ** -0.5)
    kv_shape = (B, T, Hk, dh)
    kc, vc = kc.reshape(kv_shape), vc.reshape(kv_shape)
    ks = ks.reshape(kv_shape).transpose(0, 2, 1, 3)
    vs = vs.reshape(kv_shape).transpose(0, 2, 1, 3)
    kw = kw.reshape(kv_shape).transpose(0, 2, 1, 3)
    vw = vw.reshape(kv_shape).transpose(0, 2, 1, 3)
    t_pos = jnp.arange(T)

    k_cmp = _compress(kc, pe_k, wk1, wk2)
    v_cmp = _compress(vc, pe_v, wv1, wv2)
    n_cmp = k_cmp.shape[2]
    cmp_start = jnp.arange(n_cmp) * CMP_STRIDE
    mask_c = (cmp_start + CMP_BLOCK - 1)[None, :] <= t_pos[:, None]
    p_cmp = _masked_softmax(jnp.einsum("bhgtd,bhcd->bhgtc", q, k_cmp), mask_c)
    o_cmp = jnp.einsum("bhgtc,bhcd->bhgtd", p_cmp.astype(v_cmp.dtype), v_cmp)

    n_slc = T // SLC_BLOCK
    blk = jnp.arange(n_slc)
    slc_start = blk * SLC_BLOCK
    overlap = ((cmp_start[:, None] < (slc_start + SLC_BLOCK)[None, :]) &
               ((cmp_start + CMP_BLOCK)[:, None] > slc_start[None, :])).astype(jnp.float32)
    imp = jnp.einsum("bhgtc,cn->bhtn", p_cmp, overlap)
    valid = slc_start[None, :] <= t_pos[:, None]
    d_cur = (t_pos // SLC_BLOCK)[:, None] - blk[None, :]
    forced = (blk[None, :] == 0) | ((d_cur >= 0) & (d_cur < SLC_FORCED_LOCAL))
    score = jnp.where(forced, FORCE_BONUS, jnp.where(valid, imp, NEG_INF))
    top_n = min(SLC_TOP_N, n_slc)
    _, sel = lax.top_k(score, top_n)

    ks_blk = ks.reshape(B, Hk, n_slc, SLC_BLOCK, dh)
    vs_blk = vs.reshape(B, Hk, n_slc, SLC_BLOCK, dh)
    kw_pad = jnp.pad(kw, ((0, 0), (0, 0), (WINDOW, 0), (0, 0)))
    vw_pad = jnp.pad(vw, ((0, 0), (0, 0), (WINDOW, 0), (0, 0)))
    QB = NSA_QBLK
    nq = T // QB
    q_blocks = jnp.moveaxis(q.reshape(B, Hk, G, nq, QB, dh), 3, 0)
    sel_blocks = jnp.moveaxis(sel.reshape(B, Hk, nq, QB, top_n), 2, 0)
    bi = jnp.arange(B)[:, None, None, None]
    hi = jnp.arange(Hk)[None, :, None, None]

    def local_branches(args):
        qi, q_b, sel_b = args
        t_b = qi * QB + jnp.arange(QB)
        k_sel = ks_blk[bi, hi, sel_b]
        v_sel = vs_blk[bi, hi, sel_b].reshape(B, Hk, QB, top_n * SLC_BLOCK, dh)
        pos = sel_b[..., None] * SLC_BLOCK + jnp.arange(SLC_BLOCK)
        mask_s = (pos <= t_b[:, None, None]).reshape(B, Hk, 1, QB, top_n * SLC_BLOCK)
        s_sel = jnp.einsum("bhgqd,bhqnsd->bhgqns", q_b, k_sel).reshape(B, Hk, G, QB, top_n * SLC_BLOCK)
        p_sel = _masked_softmax(s_sel, mask_s)
        o_sel = jnp.einsum("bhgqk,bhqkd->bhgqd", p_sel.astype(v_sel.dtype), v_sel)
        k_win = lax.dynamic_slice_in_dim(kw_pad, qi * QB, WINDOW + QB, axis=2)
        v_win = lax.dynamic_slice_in_dim(vw_pad, qi * QB, WINDOW + QB, axis=2)
        kpos = qi * QB - WINDOW + jnp.arange(WINDOW + QB)
        diff = t_b[:, None] - kpos[None, :]
        mask_w = (diff >= 0) & (diff < WINDOW) & (kpos[None, :] >= 0)
        p_win = _masked_softmax(jnp.einsum("bhgqd,bhkd->bhgqk", q_b, k_win), mask_w)
        o_win = jnp.einsum("bhgqk,bhkd->bhgqd", p_win.astype(v_win.dtype), v_win)
        return o_sel, o_win

    o_sel, o_win = lax.map(local_branches, (jnp.arange(nq), q_blocks, sel_blocks))
    o_sel = jnp.moveaxis(o_sel, 0, 3).reshape(B, Hk, G, T, dh)
    o_win = jnp.moveaxis(o_win, 0, 3).reshape(B, Hk, G, T, dh)

    gates = jax.nn.sigmoid(g.reshape(B, T, Hk, G, 3).transpose(0, 2, 3, 1, 4))
    o = gates[..., 0:1] * o_cmp + gates[..., 1:2] * o_sel + gates[..., 2:3] * o_win
    o = o.transpose(0, 3, 1, 2, 4).reshape(B, T, H * dh)
    return o @ w_out


def swiglu(x, w_gate, w_up, w_down):
    return (jax.nn.silu(x @ w_gate) * (x @ w_up)) @ w_down


def moe_swiglu(x, w_router, w_gate, w_up, w_down):
    logits = (x @ w_router).astype(jnp.float32)
    top_val, top_idx = lax.top_k(logits, TOP_K)
    wts = jax.nn.softmax(top_val, axis=-1)
    gates = jnp.sum(jax.nn.one_hot(top_idx, N_EXPERTS, dtype=jnp.float32) * wts[..., None], axis=-2)
    out = jnp.zeros_like(x)
    for e in range(N_EXPERTS):
        h = jax.nn.silu(x @ w_gate[e]) * (x @ w_up[e])
        out = out + gates[..., e:e + 1].astype(x.dtype) * (h @ w_down[e])
    return out


def setup_inputs(seed: int = 0) -> dict:
    key = jax.random.key(seed)
    ks = iter(jax.random.split(key, 40))
    f32 = jnp.float32
    D, dh = D_MODEL, HEAD_DIM
    nA, nB, nC = _count(N_MIXERS, 0), _count(N_MIXERS, 1), _count(N_MIXERS, 2)
    nDense, nMoE = _count(2, 0), _count(2, 1)

    def nrm(shape, scale):
        return jax.random.normal(next(ks), shape, f32) * scale

    def gain(shape):
        return 1.0 + nrm(shape, 0.02)

    beta = DEEPNORM_BETA
    return {
        "x": nrm((BATCH, SEQ, D), 1.0),
        "ln_mix_g": gain((DEPTH, D)),
        "ln_mix_b": nrm((DEPTH, D), 0.02),
        "ln_ffn_g": gain((DEPTH, D)),
        "ln_ffn_b": nrm((DEPTH, D), 0.02),
        "conf_w_in": nrm((nA, D, 2 * D), D ** -0.5),
        "conf_w_dw": nrm((nA, CONV_KERNEL, D), CONV_KERNEL ** -0.5),
        "conf_b_dw": nrm((nA, D), 0.02),
        "conf_ln_g": gain((nA, D)),
        "conf_ln_b": nrm((nA, D), 0.02),
        "conf_w_out": nrm((nA, D, D), beta * D ** -0.5),
        "sc_w_in": nrm((nB, D, 3 * D), D ** -0.5),
        "sc_w_conv": nrm((nB, SHORT_KERNEL, D), SHORT_KERNEL ** -0.5),
        "sc_w_out": nrm((nB, D, D), beta * D ** -0.5),
        "nsa_w_in": nrm((nC, D, NSA_IN_COLS), D ** -0.5),
        "nsa_pe_k": nrm((nC, CMP_BLOCK, dh), 0.1),
        "nsa_pe_v": nrm((nC, CMP_BLOCK, dh), 0.1),
        "nsa_wk1": nrm((nC, CMP_BLOCK * dh, PHI_HIDDEN), (CMP_BLOCK * dh) ** -0.5),
        "nsa_wk2": nrm((nC, PHI_HIDDEN, dh), PHI_HIDDEN ** -0.5),
        "nsa_wv1": nrm((nC, CMP_BLOCK * dh, PHI_HIDDEN), (CMP_BLOCK * dh) ** -0.5),
        "nsa_wv2": nrm((nC, PHI_HIDDEN, dh), PHI_HIDDEN ** -0.5),
        "nsa_w_out": nrm((nC, D, D), beta * D ** -0.5),
        "ffn_w_gate": nrm((nDense, D, D_FF), D ** -0.5),
        "ffn_w_up": nrm((nDense, D, D_FF), D ** -0.5),
        "ffn_w_down": nrm((nDense, D_FF, D), beta * D_FF ** -0.5),
        "moe_w_router": nrm((nMoE, D, N_EXPERTS), D ** -0.5),
        "moe_w_gate": nrm((nMoE, N_EXPERTS, D, D_FF_EXPERT), D ** -0.5),
        "moe_w_up": nrm((nMoE, N_EXPERTS, D, D_FF_EXPERT), D ** -0.5),
        "moe_w_down": nrm((nMoE, N_EXPERTS, D_FF_EXPERT, D), beta * D_FF_EXPERT ** -0.5),
    }


def reference(x, ln_mix_g, ln_mix_b, ln_ffn_g, ln_ffn_b,
              conf_w_in, conf_w_dw, conf_b_dw, conf_ln_g, conf_ln_b, conf_w_out,
              sc_w_in, sc_w_conv, sc_w_out,
              nsa_w_in, nsa_pe_k, nsa_pe_v, nsa_wk1, nsa_wk2, nsa_wv1, nsa_wv2, nsa_w_out,
              ffn_w_gate, ffn_w_up, ffn_w_down,
              moe_w_router, moe_w_gate, moe_w_up, moe_w_down):
    for i in range(DEPTH):
        kind, j = i % N_MIXERS, i // N_MIXERS
        if kind == 0:
            mix = conformer_conv(x, conf_w_in[j], conf_w_dw[j], conf_b_dw[j],
                                 conf_ln_g[j], conf_ln_b[j], conf_w_out[j])
        elif kind == 1:
            mix = short_gated_conv(x, sc_w_in[j], sc_w_conv[j], sc_w_out[j])
        else:
            mix = native_sparse_attention(x, nsa_w_in[j], nsa_pe_k[j], nsa_pe_v[j],
                                          nsa_wk1[j], nsa_wk2[j], nsa_wv1[j], nsa_wv2[j],
                                          nsa_w_out[j])
        x = _layer_norm(DEEPNORM_ALPHA * x + mix, ln_mix_g[i], ln_mix_b[i])
        f = i // 2
        if i % 2 == 0:
            ffn = swiglu(x, ffn_w_gate[f], ffn_w_up[f], ffn_w_down[f])
        else:
            ffn = moe_swiglu(x, moe_w_router[f], moe_w_gate[f], moe_w_up[f], moe_w_down[f])
        x = _layer_norm(DEEPNORM_ALPHA * x + ffn, ln_ffn_g[i], ln_ffn_b[i])
    return x
```

```python
import functools

import jax
import jax.numpy as jnp
from jax import lax
from jax.experimental import pallas as pl
from jax.experimental.pallas import tpu as pltpu

F32 = jnp.float32
BF16 = jnp.bfloat16
I32 = jnp.int32

DEPTH = 4
N_MIXERS = 3
ALPHA = (2.0 * DEPTH) ** 0.25
LN_EPS = 1e-5
NEG_INF = -1e30

CONV_KERNEL = 31
SHORT_KERNEL = 3
NSA_HEADS = 16
NSA_KV_HEADS = 4
NSA_GROUP = NSA_HEADS // NSA_KV_HEADS
CMP_BLOCK = 32
CMP_STRIDE = 16
SLC_BLOCK = 64
SLC_TOP_N = 16
SLC_FORCED_LOCAL = 2
FORCE_BONUS = 1e4
WINDOW = 512
N_EXPERTS = 8
TOP_K = 2

LANES = 128
SUBLANES = 8
VMEM_LIMIT_BYTES = 56 * 1024 * 1024

HALO = 32
CONV_ROWS = 16
SEL_PENALTY = -1e9


def _nt(a, b, **kw):
    return lax.dot_general(a, b, (((1,), (1,)), ((), ())), preferred_element_type=F32, **kw)


def _mm(a, b):
    return jnp.dot(a, b, preferred_element_type=F32)


def _layer_norm(v, g, b):
    mu = jnp.mean(v, -1, keepdims=True)
    d = v - mu
    var = jnp.mean(d * d, -1, keepdims=True)
    return d * lax.rsqrt(var + LN_EPS) * g + b


def _full(shape):
    return pl.BlockSpec(shape, lambda *_: (0,) * len(shape), pipeline_mode=pl.Buffered(1))


def _params(semantics):
    return pltpu.CompilerParams(dimension_semantics=semantics, vmem_limit_bytes=VMEM_LIMIT_BYTES)


def _route(y, wr_t, carry_ref, idx_ref, rank_ref, wts_ref, cnt_ref):
    tm = y.shape[0]
    logits = _nt(wr_t, y, precision=lax.Precision.HIGHEST)
    eidx = lax.broadcasted_iota(I32, logits.shape, 0).astype(F32)
    m1 = jnp.max(logits, axis=0, keepdims=True)
    i1 = jnp.min(jnp.where(logits == m1, eidx, float(N_EXPERTS)), axis=0, keepdims=True)
    rest = jnp.where(eidx == i1, -jnp.inf, logits)
    m2 = jnp.max(rest, axis=0, keepdims=True)
    i2 = jnp.min(jnp.where(rest == m2, eidx, float(N_EXPERTS)), axis=0, keepdims=True)
    e2 = jnp.exp(m2 - m1)
    denom = 1.0 + e2
    chosen = (eidx == i1) | (eidx == i2)
    chosen_f = chosen.astype(F32)
    upper = (lax.broadcasted_iota(I32, (tm, tm), 0) < lax.broadcasted_iota(I32, (tm, tm), 1)).astype(BF16)
    before = _mm(chosen_f.astype(BF16), upper) + carry_ref[...]
    r1 = jnp.sum(jnp.where(eidx == i1, before, 0.0), axis=0, keepdims=True)
    r2 = jnp.sum(jnp.where(eidx == i2, before, 0.0), axis=0, keepdims=True)
    carry_ref[...] = carry_ref[...] + jnp.sum(chosen_f, axis=1, keepdims=True)
    idx_ref[...] = jnp.concatenate([i1, i2], axis=0).astype(I32)
    rank_ref[...] = jnp.concatenate([r1, r2], axis=0).astype(I32)
    wts_ref[...] = jnp.concatenate([1.0 / denom, e2 / denom], axis=0)
    cnt_ref[...] = jnp.broadcast_to(carry_ref[...], cnt_ref.shape)


def _router_out(n, tm):
    shapes = [jax.ShapeDtypeStruct((TOP_K, n), I32), jax.ShapeDtypeStruct((TOP_K, n), I32),
              jax.ShapeDtypeStruct((TOP_K, n), F32), jax.ShapeDtypeStruct((N_EXPERTS, LANES), F32)]
    specs = [pl.BlockSpec((TOP_K, tm), lambda i: (0, i))] * 3 + [pl.BlockSpec((N_EXPERTS, LANES), lambda i: (0, 0))]
    return shapes, specs


def _causal_conv(ubuf, wdw_ref, out_ref, bias, tm, ksize):
    d = out_ref.shape[-1]

    def chunk(c, _):
        r0 = pl.multiple_of(c * CONV_ROWS, CONV_ROWS)
        win = ubuf[pl.ds(r0, CONV_ROWS + HALO), :]
        acc = jnp.broadcast_to(bias, (CONV_ROWS, d))
        for k in range(ksize):
            off = HALO - (ksize - 1) + k
            acc = acc + wdw_ref[k:k + 1, :] * win[off:off + CONV_ROWS, :]
        out_ref[pl.ds(r0, CONV_ROWS), :] = acc
        return 0

    lax.fori_loop(0, tm // CONV_ROWS, chunk, 0)


def _mixer_kernel(*refs, kind, tm, tiles_per_seq, with_router):
    if kind == "conformer":
        (x_ref, win_ref, wdw_ref, bdw_ref, cg_ref, cb_ref, wout_ref, g_ref, b_ref), refs = refs[:9], refs[9:]
    else:
        (x_ref, win_ref, wdw_ref, wout_ref, g_ref, b_ref), refs = refs[:6], refs[6:]
    if with_router:
        wr_ref, refs = refs[0], refs[1:]
        out_ref, idx_ref, rank_ref, wts_ref, cnt_ref = refs[:5]
        winb, woutb, ubuf, cbuf, carry = refs[5:]
    else:
        out_ref = refs[0]
        winb, woutb, ubuf, cbuf = refs[1:]
    i = pl.program_id(0)
    d = x_ref.shape[-1]

    @pl.when(i == 0)
    def _():
        winb[...] = win_ref[...].astype(BF16)
        woutb[...] = wout_ref[...].astype(BF16)
        if with_router:
            carry[...] = jnp.zeros_like(carry)

    @pl.when(i % tiles_per_seq == 0)
    def _():
        ubuf[0:HALO, :] = jnp.zeros((HALO, d), F32)

    x = x_ref[...]
    h = _mm(x.astype(BF16), winb[...])
    if kind == "conformer":
        ubuf[HALO:HALO + tm, :] = h[:, :d] * jax.nn.sigmoid(h[:, d:])
        _causal_conv(ubuf, wdw_ref, cbuf, bdw_ref[...], tm, CONV_KERNEL)
        v = jax.nn.silu(_layer_norm(cbuf[...], cg_ref[...], cb_ref[...]))
    else:
        ubuf[HALO:HALO + tm, :] = h[:, d:2 * d] * h[:, 2 * d:]
        _causal_conv(ubuf, wdw_ref, cbuf, jnp.zeros((1, d), F32), tm, SHORT_KERNEL)
        v = h[:, :d] * cbuf[...]
    ubuf[0:HALO, :] = ubuf[tm:tm + HALO, :]
    mix = _mm(v.astype(BF16), woutb[...])
    y = _layer_norm(ALPHA * x + mix, g_ref[...], b_ref[...])
    out_ref[...] = y
    if with_router:
        _route(y, wr_ref[...], carry, idx_ref, rank_ref, wts_ref, cnt_ref)


def _mixer(kind, x, weights, ln_g, ln_b, seq_len, w_router=None):
    n, d = x.shape
    tm = 512
    assert seq_len % tm == 0 and n % seq_len == 0
    with_router = w_router is not None
    if kind == "conformer":
        w_in, w_dw, b_dw, cg, cb, w_out = weights
        ins = [x, w_in, w_dw, b_dw[None], cg[None], cb[None], w_out, ln_g[None], ln_b[None]]
    else:
        w_in, w_dw, w_out = weights
        ins = [x, w_in, w_dw, w_out, ln_g[None], ln_b[None]]
    in_specs = [pl.BlockSpec((tm, d), lambda i: (i, 0))] + [_full(a.shape) for a in ins[1:]]
    out_shape = [jax.ShapeDtypeStruct((n, d), F32)]
    out_specs = [pl.BlockSpec((tm, d), lambda i: (i, 0))]
    scratch = [pltpu.VMEM(w_in.shape, BF16), pltpu.VMEM(w_out.shape, BF16),
               pltpu.VMEM((tm + HALO, d), F32), pltpu.VMEM((tm, d), F32)]
    if with_router:
        wr_t = w_router.T
        ins.append(wr_t)
        in_specs.append(_full(wr_t.shape))
        rs, rspec = _router_out(n, tm)
        out_shape += rs
        out_specs += rspec
        scratch.append(pltpu.VMEM((N_EXPERTS, 1), F32))
    return pl.pallas_call(
        functools.partial(_mixer_kernel, kind=kind, tm=tm, tiles_per_seq=seq_len // tm, with_router=with_router),
        grid=(n // tm,), in_specs=in_specs, out_specs=out_specs, out_shape=out_shape,
        scratch_shapes=scratch, compiler_params=_params(("arbitrary",)), name=f"mixer_{kind}",
    )(*ins)


def _ffn_kernel(x_ref, wg_ref, wu_ref, wd_ref, g_ref, b_ref, out_ref, xb, acc):
    j = pl.program_id(1)

    @pl.when(j == 0)
    def _():
        xb[...] = x_ref[...].astype(BF16)
        acc[...] = jnp.zeros_like(acc)

    hg = _mm(xb[...], wg_ref[...].astype(BF16))
    hu = _mm(xb[...], wu_ref[...].astype(BF16))
    acc[...] += _mm((jax.nn.silu(hg) * hu).astype(BF16), wd_ref[...].astype(BF16))

    @pl.when(j == pl.num_programs(1) - 1)
    def _():
        out_ref[...] = _layer_norm(ALPHA * x_ref[...] + acc[...], g_ref[...], b_ref[...])


def _dense_ffn(x, w_gate, w_up, w_down, ln_g, ln_b):
    n, d = x.shape
    dff = w_gate.shape[1]
    tm, tf = 1024, 256
    assert n % tm == 0 and dff % tf == 0
    return pl.pallas_call(
        _ffn_kernel, grid=(n // tm, dff // tf),
        in_specs=[pl.BlockSpec((tm, d), lambda i, j: (i, 0)),
                  pl.BlockSpec((d, tf), lambda i, j: (0, j)),
                  pl.BlockSpec((d, tf), lambda i, j: (0, j)),
                  pl.BlockSpec((tf, d), lambda i, j: (j, 0)),
                  pl.BlockSpec((1, d), lambda i, j: (0, 0)),
                  pl.BlockSpec((1, d), lambda i, j: (0, 0))],
        out_specs=pl.BlockSpec((tm, d), lambda i, j: (i, 0)),
        out_shape=jax.ShapeDtypeStruct((n, d), F32),
        scratch_shapes=[pltpu.VMEM((tm, d), BF16), pltpu.VMEM((tm, d), F32)],
        compiler_params=_params(("arbitrary", "arbitrary")), name="dense_ffn",
    )(x, w_gate, w_up, w_down, ln_g[None], ln_b[None])


MOE_TM = 1024
MOE_TF = 512
MOE_TB = 256


def _row_copy(src, dst, s, t, sem):
    return pltpu.make_async_copy(src.at[pl.ds(s, 1), :], dst.at[pl.ds(t, 1), :], sem)


def _dispatch_kernel(d0_ref, d1_ref, x_hbm, zeros_hbm, xs_hbm, sem):
    del zeros_hbm
    base = pl.program_id(0) * MOE_TB

    def start(t, _):
        _row_copy(x_hbm, xs_hbm, base + t, d0_ref[base + t], sem).start()
        _row_copy(x_hbm, xs_hbm, base + t, d1_ref[base + t], sem).start()
        return 0

    def wait(t, _):
        _row_copy(x_hbm, xs_hbm, base + t, d0_ref[base + t], sem).wait()
        _row_copy(x_hbm, xs_hbm, base + t, d1_ref[base + t], sem).wait()
        return 0

    lax.fori_loop(0, MOE_TB, start, 0)
    lax.fori_loop(0, MOE_TB, wait, 0)


def _dispatch(x, dest0, dest1, rows):
    n, d = x.shape
    return pl.pallas_call(
        _dispatch_kernel,
        grid_spec=pltpu.PrefetchScalarGridSpec(
            num_scalar_prefetch=2, grid=(n // MOE_TB,),
            in_specs=[pl.BlockSpec(memory_space=pl.ANY), pl.BlockSpec(memory_space=pl.ANY)],
            out_specs=pl.BlockSpec(memory_space=pl.ANY),
            scratch_shapes=[pltpu.SemaphoreType.DMA(())]),
        out_shape=jax.ShapeDtypeStruct((rows, d), F32),
        input_output_aliases={3: 0},
        compiler_params=_params(("arbitrary",)), name="moe_dispatch",
    )(dest0, dest1, x, jnp.zeros((rows, d), F32))


def _expert_kernel(te_ref, tx_ref, tv_ref, x_ref, wg_ref, wu_ref, wd_ref, out_ref, xb, acc):
    del te_ref, tx_ref
    i, j = pl.program_id(0), pl.program_id(1)

    @pl.when(tv_ref[i] == 1)
    def _():
        @pl.when(j == 0)
        def _():
            xb[...] = x_ref[...].astype(BF16)
            acc[...] = jnp.zeros_like(acc)

        hg = _mm(xb[...], wg_ref[0].astype(BF16))
        hu = _mm(xb[...], wu_ref[0].astype(BF16))
        acc[...] += _mm((jax.nn.silu(hg) * hu).astype(BF16), wd_ref[0].astype(BF16))

        @pl.when(j == pl.num_programs(1) - 1)
        def _():
            out_ref[...] = acc[...]

    @pl.when((tv_ref[i] == 0) & (j == pl.num_programs(1) - 1))
    def _():
        out_ref[...] = jnp.zeros_like(out_ref)


def _experts(xs, tile_expert, tile_row, tile_valid, w_gate, w_up, w_down):
    rows, d = xs.shape
    dff = w_gate.shape[2]
    n_ff = dff // MOE_TF
    assert dff % MOE_TF == 0 and rows % MOE_TM == 0

    def jj(i, j, tv):
        return jnp.where(tv[i] == 1, j, n_ff - 1)

    return pl.pallas_call(
        _expert_kernel,
        grid_spec=pltpu.PrefetchScalarGridSpec(
            num_scalar_prefetch=3, grid=(rows // MOE_TM, n_ff),
            in_specs=[pl.BlockSpec((MOE_TM, d), lambda i, j, te, tx, tv: (tx[i], 0)),
                      pl.BlockSpec((1, d, MOE_TF), lambda i, j, te, tx, tv: (te[i], 0, jj(i, j, tv))),
                      pl.BlockSpec((1, d, MOE_TF), lambda i, j, te, tx, tv: (te[i], 0, jj(i, j, tv))),
                      pl.BlockSpec((1, MOE_TF, d), lambda i, j, te, tx, tv: (te[i], jj(i, j, tv), 0))],
            out_specs=pl.BlockSpec((MOE_TM, d), lambda i, j, te, tx, tv: (i, 0)),
            scratch_shapes=[pltpu.VMEM((MOE_TM, d), BF16), pltpu.VMEM((MOE_TM, d), F32)]),
        out_shape=jax.ShapeDtypeStruct((rows, d), F32),
        compiler_params=_params(("arbitrary", "arbitrary")), name="moe_experts",
    )(tile_expert, tile_row, tile_valid, xs, w_gate, w_up, w_down)


def _combine_kernel(d0_ref, d1_ref, x_ref, w_ref, g_ref, b_ref, ys_hbm, out_ref, ybuf, sem):
    base = pl.program_id(0) * MOE_TB

    def start(t, _):
        _row_copy(ys_hbm, ybuf.at[0], d0_ref[base + t], t, sem).start()
        _row_copy(ys_hbm, ybuf.at[1], d1_ref[base + t], t, sem).start()
        return 0

    def wait(t, _):
        _row_copy(ys_hbm, ybuf.at[0], d0_ref[base + t], t, sem).wait()
        _row_copy(ys_hbm, ybuf.at[1], d1_ref[base + t], t, sem).wait()
        return 0

    lax.fori_loop(0, MOE_TB, start, 0)
    lax.fori_loop(0, MOE_TB, wait, 0)
    w = w_ref[...]
    ffn = w[:, 0:1] * ybuf[0] + w[:, 1:2] * ybuf[1]
    out_ref[...] = _layer_norm(ALPHA * x_ref[...] + ffn, g_ref[...], b_ref[...])


def _combine(x, ys, dest0, dest1, wts, ln_g, ln_b):
    n, d = x.shape
    return pl.pallas_call(
        _combine_kernel,
        grid_spec=pltpu.PrefetchScalarGridSpec(
            num_scalar_prefetch=2, grid=(n // MOE_TB,),
            in_specs=[pl.BlockSpec((MOE_TB, d), lambda i, d0, d1: (i, 0)),
                      pl.BlockSpec((MOE_TB, TOP_K), lambda i, d0, d1: (i, 0)),
                      pl.BlockSpec((1, d), lambda i, d0, d1: (0, 0)),
                      pl.BlockSpec((1, d), lambda i, d0, d1: (0, 0)),
                      pl.BlockSpec(memory_space=pl.ANY)],
            out_specs=pl.BlockSpec((MOE_TB, d), lambda i, d0, d1: (i, 0)),
            scratch_shapes=[pltpu.VMEM((TOP_K, MOE_TB, d), F32), pltpu.SemaphoreType.DMA(())]),
        out_shape=jax.ShapeDtypeStruct((n, d), F32),
        compiler_params=_params(("arbitrary",)), name="moe_combine",
    )(dest0, dest1, x, wts, ln_g[None], ln_b[None], ys)


def _moe(x, idx, rank, wts, counts, w_gate, w_up, w_down, ln_g, ln_b):
    n, _ = x.shape
    n_tiles = (TOP_K * n) // MOE_TM + N_EXPERTS
    rows = n_tiles * MOE_TM
    cnt = counts.astype(I32)
    padded = ((cnt + MOE_TM - 1) // MOE_TM) * MOE_TM
    ends = jnp.cumsum(padded)
    offsets = ends - padded
    dest = offsets[idx] + rank
    tile_start = jnp.arange(n_tiles, dtype=I32) * MOE_TM
    tile_valid = (tile_start < ends[-1]).astype(I32)
    last = jnp.maximum(ends[-1] // MOE_TM - 1, 0)
    tile_row = jnp.minimum(jnp.arange(n_tiles, dtype=I32), last)
    tile_expert = jnp.minimum(jnp.sum(tile_row[:, None] * MOE_TM >= ends[None, :], axis=1), N_EXPERTS - 1).astype(I32)
    xs = _dispatch(x, dest[0], dest[1], rows)
    ys = _experts(xs, tile_expert, tile_row, tile_valid, w_gate, w_up, w_down)
    return _combine(x, ys, dest[0], dest[1], wts.T, ln_g, ln_b)


def _nsa_proj_kernel(x_ref, win_ref, q_ref, kv_ref, gate_ref, winb, *, qd, kvd, scale):
    @pl.when(pl.program_id(0) == 0)
    def _():
        winb[...] = win_ref[...].astype(BF16)

    h = _mm(x_ref[...].astype(BF16), winb[...])
    q_ref[...] = h[:, :qd] * scale
    kv_ref[...] = h[:, qd:qd + kvd]
    gate_ref[...] = jax.nn.sigmoid(h[:, qd + kvd:])


def _nsa_proj(x, w_in, qd, kvd):
    n, d = x.shape
    cols = w_in.shape[1]
    gd = cols - qd - kvd
    tm = 512
    dh = qd // NSA_HEADS
    return pl.pallas_call(
        functools.partial(_nsa_proj_kernel, qd=qd, kvd=kvd, scale=dh ** -0.5),
        grid=(n // tm,),
        in_specs=[pl.BlockSpec((tm, d), lambda i: (i, 0)), _full(w_in.shape)],
        out_specs=[pl.BlockSpec((tm, qd), lambda i: (i, 0)), pl.BlockSpec((tm, kvd), lambda i: (i, 0)),
                   pl.BlockSpec((tm, gd), lambda i: (i, 0))],
        out_shape=[jax.ShapeDtypeStruct((n, qd), F32), jax.ShapeDtypeStruct((n, kvd), F32),
                   jax.ShapeDtypeStruct((n, gd), F32)],
        scratch_shapes=[pltpu.VMEM(w_in.shape, BF16)],
        compiler_params=_params(("arbitrary",)), name="nsa_proj",
    )(x, w_in)


def _compress_kernel(ck_ref, cv_ref, pek_ref, pev_ref, wk1_ref, wk1c_ref, wk2_ref, wv1_ref, wv1c_ref, wv2_ref,
                     ko_ref, vo_ref):
    def one(c_ref, pe_ref, w1_ref, w1c_ref, w2_ref, o_ref):
        hid = w2_ref.shape[0]
        a = _mm(c_ref[0].astype(BF16), w1c_ref[...].astype(BF16))
        bias = _mm(pe_ref[...].astype(BF16), w1_ref[...].astype(BF16))[0:1, :]
        pre = a[:, :hid] + pltpu.roll(a[:, hid:], a.shape[0] - 1, 0) + bias
        o_ref[0] = _mm(jax.nn.gelu(pre).astype(BF16), w2_ref[...].astype(BF16))

    one(ck_ref, pek_ref, wk1_ref, wk1c_ref, wk2_ref, ko_ref)
    one(cv_ref, pev_ref, wv1_ref, wv1c_ref, wv2_ref, vo_ref)


def _compress(ck, cv, pe_k, pe_v, wk1, wk2, wv1, wv2):
    bh, chunks, cw = ck.shape
    dh = wk2.shape[1]

    def prep(pe, w1):
        pe_rows = jnp.zeros((SUBLANES, pe.size), F32).at[0].set(pe.reshape(-1))
        return pe_rows, jnp.concatenate([w1[:cw], w1[cw:]], axis=1)

    pek, wk1c = prep(pe_k, wk1)
    pev, wv1c = prep(pe_v, wv1)
    ins = [ck, cv, pek, pev, wk1, wk1c, wk2, wv1, wv1c, wv2]
    blk = pl.BlockSpec((1, chunks, cw), lambda i: (i, 0, 0))
    oblk = pl.BlockSpec((1, chunks, dh), lambda i: (i, 0, 0))
    return pl.pallas_call(
        _compress_kernel, grid=(bh,),
        in_specs=[blk, blk] + [_full(a.shape) for a in ins[2:]],
        out_specs=[oblk, oblk],
        out_shape=[jax.ShapeDtypeStruct((bh, chunks, dh), F32)] * 2,
        compiler_params=_params(("arbitrary",)), name="nsa_compress",
    )(*ins)


NSA_TQ = 256
NSA_TK = 256


def _masked_softmax(s, mask):
    s = jnp.where(mask, s, NEG_INF)
    m = jnp.max(s, -1, keepdims=True)
    e = jnp.exp(s - m) * mask.astype(F32)
    return e / jnp.maximum(jnp.sum(e, -1, keepdims=True), 1e-30)


def _attn_kernel(q_ref, gate_ref, kc_ref, vc_ref, ks_ref, vs_ref, kw_ref, vw_ref, o_ref, m_scr, l_scr, acc_scr):
    g, tq, dh = q_ref.shape[1:]
    m_rows = g * tq
    n_cmp_pad = kc_ref.shape[1]
    n_slc = ks_ref.shape[1] // SLC_BLOCK
    qi = pl.program_id(1)
    t0 = qi * tq
    qb = q_ref[0].reshape(m_rows, dh).astype(BF16)
    t_col = t0 + lax.broadcasted_iota(I32, (tq, 1), 0)
    t_rows = jnp.concatenate([t_col] * g, axis=0)

    c_idx = lax.broadcasted_iota(I32, (tq, n_cmp_pad), 1)
    mask_c = (c_idx * CMP_STRIDE + CMP_BLOCK - 1 <= t_col) & (c_idx < n_cmp_pad - 1)
    s_c = _nt(qb, kc_ref[0].astype(BF16)).reshape(g, tq, n_cmp_pad)
    p_c = _masked_softmax(s_c, jnp.broadcast_to(mask_c[None], s_c.shape))
    o_cmp = _mm(p_c.reshape(m_rows, n_cmp_pad).astype(BF16), vc_ref[0].astype(BF16))

    p_sum = jnp.sum(p_c, axis=0)
    n_i = lax.broadcasted_iota(I32, (n_slc, n_cmp_pad), 0) * SLC_BLOCK
    c_s = lax.broadcasted_iota(I32, (n_slc, n_cmp_pad), 1) * CMP_STRIDE
    overlap_t = ((c_s < n_i + SLC_BLOCK) & (c_s + CMP_BLOCK > n_i)).astype(F32)
    imp_t = _nt(overlap_t, p_sum, precision=lax.Precision.HIGHEST)
    blk = lax.broadcasted_iota(I32, (n_slc, tq), 0)
    t_lane = t0 + lax.broadcasted_iota(I32, (n_slc, tq), 1)
    d_cur = t_lane // SLC_BLOCK - blk
    forced = (blk == 0) | ((d_cur >= 0) & (d_cur < SLC_FORCED_LOCAL))
    score = jnp.where(forced, FORCE_BONUS, jnp.where(blk * SLC_BLOCK <= t_lane, imp_t, NEG_INF))
    rank = jnp.zeros((n_slc, tq), I32)
    for m in range(n_slc):
        sm = score[m:m + 1, :]
        rank = rank + ((sm > score) | ((sm == score) & (blk > m))).astype(I32)
    pen_t = jnp.where(rank < min(SLC_TOP_N, n_slc), 0.0, SEL_PENALTY).astype(BF16)
    eye = (lax.broadcasted_iota(I32, (tq, tq), 0) == lax.broadcasted_iota(I32, (tq, tq), 1)).astype(BF16)
    pen = _nt(eye, pen_t).astype(BF16)
    pen_rows = jnp.concatenate([pen] * g, axis=0)

    m_scr[...] = jnp.full(m_scr.shape, NEG_INF, F32)
    l_scr[...] = jnp.zeros_like(l_scr)
    acc_scr[...] = jnp.zeros_like(acc_scr)

    def sel_step(kt, _):
        k0 = pl.multiple_of(kt * NSA_TK, NSA_TK)
        kb = ks_ref[0, pl.ds(k0, NSA_TK), :].astype(BF16)
        vb = vs_ref[0, pl.ds(k0, NSA_TK), :].astype(BF16)
        j_pos = k0 + lax.broadcasted_iota(I32, (n_slc, NSA_TK), 1)
        expand = (j_pos // SLC_BLOCK == lax.broadcasted_iota(I32, (n_slc, NSA_TK), 0)).astype(BF16)
        s = _nt(qb, kb) + _mm(pen_rows, expand)
        key_pos = k0 + lax.broadcasted_iota(I32, (m_rows, NSA_TK), 1)
        s = jnp.where(key_pos <= t_rows, s, NEG_INF)
        m_old = m_scr[...]
        m_new = jnp.maximum(m_old, jnp.max(s, -1, keepdims=True))
        a = jnp.exp(m_old - m_new)
        p = jnp.exp(s - m_new)
        l_scr[...] = a * l_scr[...] + jnp.sum(p, -1, keepdims=True)
        acc_scr[...] = a * acc_scr[...] + _mm(p.astype(BF16), vb)
        m_scr[...] = m_new
        return 0

    lax.fori_loop(0, (t0 + tq) // NSA_TK, sel_step, 0)
    o_sel = acc_scr[...] / l_scr[...]

    span = WINDOW + tq
    kwin = kw_ref[0, pl.ds(pl.multiple_of(t0, tq), span), :].astype(BF16)
    vwin = vw_ref[0, pl.ds(pl.multiple_of(t0, tq), span), :].astype(BF16)
    kpos = t0 - WINDOW + lax.broadcasted_iota(I32, (m_rows, span), 1)
    diff = t_rows - kpos
    p_w = _masked_softmax(_nt(qb, kwin), (diff >= 0) & (diff < WINDOW) & (kpos >= 0))
    o_win = _mm(p_w.astype(BF16), vwin)

    gate = gate_ref[0].reshape(m_rows, 3)
    o = gate[:, 0:1] * o_cmp + gate[:, 1:2] * o_sel + gate[:, 2:3] * o_win
    o_ref[0] = o.reshape(g, tq, dh)


def _attention(q5, gates, k_cmp, v_cmp, ks, vs, kw_pad, vw_pad):
    bh, g, t, dh = q5.shape
    tq = NSA_TQ
    n_cmp_pad = k_cmp.shape[1]
    m_rows = g * tq
    qblk = pl.BlockSpec((1, g, tq, dh), lambda b, i: (b, 0, i, 0))
    cblk = pl.BlockSpec((1, n_cmp_pad, dh), lambda b, i: (b, 0, 0))
    kblk = pl.BlockSpec((1, t, dh), lambda b, i: (b, 0, 0))
    wblk = pl.BlockSpec((1, t + WINDOW, dh), lambda b, i: (b, 0, 0))
    return pl.pallas_call(
        _attn_kernel, grid=(bh, t // tq),
        in_specs=[qblk, pl.BlockSpec((1, g, tq, 3), lambda b, i: (b, 0, i, 0)), cblk, cblk, kblk, kblk, wblk, wblk],
        out_specs=qblk,
        out_shape=jax.ShapeDtypeStruct((bh, g, t, dh), F32),
        scratch_shapes=[pltpu.VMEM((m_rows, 1), F32), pltpu.VMEM((m_rows, 1), F32), pltpu.VMEM((m_rows, dh), F32)],
        compiler_params=_params(("arbitrary", "arbitrary")), name="nsa_attention",
    )(q5, gates, k_cmp, v_cmp, ks, vs, kw_pad, vw_pad)


def _outproj_kernel(o_ref, x_ref, w_ref, g_ref, b_ref, out_ref, wb):
    @pl.when(pl.program_id(0) == 0)
    def _():
        wb[...] = w_ref[...].astype(BF16)

    mix = _mm(o_ref[...].astype(BF16), wb[...])
    out_ref[...] = _layer_norm(ALPHA * x_ref[...] + mix, g_ref[...], b_ref[...])


def _outproj(o, x, w_out, ln_g, ln_b):
    n, d = x.shape
    tm = 512
    blk = pl.BlockSpec((tm, d), lambda i: (i, 0))
    return pl.pallas_call(
        _outproj_kernel, grid=(n // tm,),
        in_specs=[blk, blk, _full(w_out.shape), _full((1, d)), _full((1, d))],
        out_specs=blk, out_shape=jax.ShapeDtypeStruct((n, d), F32),
        scratch_shapes=[pltpu.VMEM(w_out.shape, BF16)],
        compiler_params=_params(("arbitrary",)), name="nsa_outproj",
    )(o, x, w_out, ln_g[None], ln_b[None])


def _nsa_mixer(x, batch, seq, w_in, pe_k, pe_v, wk1, wk2, wv1, wv2, w_out, ln_g, ln_b):
    n, d = x.shape
    h, hk, g = NSA_HEADS, NSA_KV_HEADS, NSA_GROUP
    dh = d // h
    qd, kd = h * dh, hk * dh
    q, kv, gates = _nsa_proj(x, w_in, qd, 6 * kd)
    q5 = q.reshape(batch, seq, hk, g, dh).transpose(0, 2, 3, 1, 4).reshape(batch * hk, g, seq, dh)
    gates = gates.reshape(batch, seq, hk, g, 3).transpose(0, 2, 3, 1, 4).reshape(batch * hk, g, seq, 3)
    kv = kv.reshape(batch, seq, 6, hk, dh).transpose(2, 0, 3, 1, 4).reshape(6, batch * hk, seq, dh)
    kc, vc, ks, vs, kw, vw = (kv[i] for i in range(6))
    chunks = seq // CMP_STRIDE
    k_cmp, v_cmp = _compress(kc.reshape(batch * hk, chunks, CMP_STRIDE * dh), vc.reshape(batch * hk, chunks, CMP_STRIDE * dh),
                             pe_k, pe_v, wk1, wk2, wv1, wv2)
    pad = ((0, 0), (WINDOW, 0), (0, 0))
    o5 = _attention(q5, gates, k_cmp, v_cmp, ks, vs, jnp.pad(kw, pad), jnp.pad(vw, pad))
    o = o5.reshape(batch, hk, g, seq, dh).transpose(0, 3, 1, 2, 4).reshape(n, d)
    return _outproj(o, x, w_out, ln_g, ln_b)


def kernel(x, ln_mix_g, ln_mix_b, ln_ffn_g, ln_ffn_b, conf_w_in, conf_w_dw, conf_b_dw, conf_ln_g, conf_ln_b, conf_w_out, sc_w_in, sc_w_conv, sc_w_out, nsa_w_in, nsa_pe_k, nsa_pe_v, nsa_wk1, nsa_wk2, nsa_wv1, nsa_wv2, nsa_w_out, ffn_w_gate, ffn_w_up, ffn_w_down, moe_w_router, moe_w_gate, moe_w_up, moe_w_down):
    batch, seq, d = x.shape
    h = x.reshape(batch * seq, d)
    for i in range(DEPTH):
        kind, j = i % N_MIXERS, i // N_MIXERS
        f = i // 2
        w_router = moe_w_router[f] if i % 2 == 1 else None
        if kind == 0:
            res = _mixer("conformer", h, (conf_w_in[j], conf_w_dw[j], conf_b_dw[j], conf_ln_g[j], conf_ln_b[j], conf_w_out[j]),
                         ln_mix_g[i], ln_mix_b[i], seq, w_router)
        elif kind == 1:
            res = _mixer("shortconv", h, (sc_w_in[j], sc_w_conv[j], sc_w_out[j]), ln_mix_g[i], ln_mix_b[i], seq, w_router)
        else:
            assert w_router is None
            res = [_nsa_mixer(h, batch, seq, nsa_w_in[j], nsa_pe_k[j], nsa_pe_v[j], nsa_wk1[j], nsa_wk2[j],
                              nsa_wv1[j], nsa_wv2[j], nsa_w_out[j], ln_mix_g[i], ln_mix_b[i])]
        h = res[0]
        if i % 2 == 0:
            h = _dense_ffn(h, ffn_w_gate[f], ffn_w_up[f], ffn_w_down[f], ln_ffn_g[i], ln_ffn_b[i])
        else:
            _, idx, rank, wts, cnt = res
            h = _moe(h, idx, rank, wts, cnt[:, 0], moe_w_gate[f], moe_w_up[f], moe_w_down[f], ln_ffn_g[i], ln_ffn_b[i])
    return h.reshape(batch, seq, d)
```

```python
import functools

import jax
import jax.numpy as jnp
from jax import lax
from jax.experimental import pallas as pl
from jax.experimental.pallas import tpu as pltpu

F32 = jnp.float32
BF16 = jnp.bfloat16
I32 = jnp.int32

DEPTH = 4
N_MIXERS = 3
ALPHA = (2.0 * DEPTH) ** 0.25
LN_EPS = 1e-5
NEG_INF = -1e30

CONV_KERNEL = 31
SHORT_KERNEL = 3
NSA_HEADS = 16
NSA_KV_HEADS = 4
NSA_GROUP = NSA_HEADS // NSA_KV_HEADS
CMP_BLOCK = 32
CMP_STRIDE = 16
SLC_BLOCK = 64
SLC_TOP_N = 16
SLC_FORCED_LOCAL = 2
FORCE_BONUS = 1e4
WINDOW = 512
N_EXPERTS = 8
TOP_K = 2

LANES = 128
SUBLANES = 8
VMEM_LIMIT_BYTES = 56 * 1024 * 1024

HALO = 32
CONV_ROWS = 16
SEL_PENALTY = -1e9


def _nt(a, b, **kw):
    return lax.dot_general(a, b, (((1,), (1,)), ((), ())), preferred_element_type=F32, **kw)


def _mm(a, b):
    return jnp.dot(a, b, preferred_element_type=F32)


def _layer_norm(v, g, b):
    mu = jnp.mean(v, -1, keepdims=True)
    d = v - mu
    var = jnp.mean(d * d, -1, keepdims=True)
    return d * lax.rsqrt(var + LN_EPS) * g + b


def _full(shape):
    return pl.BlockSpec(shape, lambda *_: (0,) * len(shape), pipeline_mode=pl.Buffered(1))


def _params(semantics):
    return pltpu.CompilerParams(dimension_semantics=semantics, vmem_limit_bytes=VMEM_LIMIT_BYTES)


def _route(y, wr_t, carry_ref, idx_ref, rank_ref, wts_ref, cnt_ref):
    tm = y.shape[0]
    logits = _nt(wr_t, y, precision=lax.Precision.HIGHEST)
    eidx = lax.broadcasted_iota(I32, logits.shape, 0).astype(F32)
    m1 = jnp.max(logits, axis=0, keepdims=True)
    i1 = jnp.min(jnp.where(logits == m1, eidx, float(N_EXPERTS)), axis=0, keepdims=True)
    rest = jnp.where(eidx == i1, -jnp.inf, logits)
    m2 = jnp.max(rest, axis=0, keepdims=True)
    i2 = jnp.min(jnp.where(rest == m2, eidx, float(N_EXPERTS)), axis=0, keepdims=True)
    e2 = jnp.exp(m2 - m1)
    denom = 1.0 + e2
    chosen = (eidx == i1) | (eidx == i2)
    chosen_f = chosen.astype(F32)
    upper = (lax.broadcasted_iota(I32, (tm, tm), 0) < lax.broadcasted_iota(I32, (tm, tm), 1)).astype(BF16)
    before = _mm(chosen_f.astype(BF16), upper) + carry_ref[...]
    r1 = jnp.sum(jnp.where(eidx == i1, before, 0.0), axis=0, keepdims=True)
    r2 = jnp.sum(jnp.where(eidx == i2, before, 0.0), axis=0, keepdims=True)
    carry_ref[...] = carry_ref[...] + jnp.sum(chosen_f, axis=1, keepdims=True)
    idx_ref[...] = jnp.concatenate([i1, i2], axis=0).astype(I32)
    rank_ref[...] = jnp.concatenate([r1, r2], axis=0).astype(I32)
    wts_ref[...] = jnp.concatenate([1.0 / denom, e2 / denom], axis=0)
    cnt_ref[...] = jnp.broadcast_to(carry_ref[...], cnt_ref.shape)


def _router_out(n, tm):
    shapes = [jax.ShapeDtypeStruct((TOP_K, n), I32), jax.ShapeDtypeStruct((TOP_K, n), I32),
              jax.ShapeDtypeStruct((TOP_K, n), F32), jax.ShapeDtypeStruct((N_EXPERTS, LANES), F32)]
    specs = [pl.BlockSpec((TOP_K, tm), lambda i: (0, i))] * 3 + [pl.BlockSpec((N_EXPERTS, LANES), lambda i: (0, 0))]
    return shapes, specs


def _causal_conv(ubuf, wdw_ref, out_ref, bias, tm, ksize):
    d = out_ref.shape[-1]

    def chunk(c, _):
        r0 = pl.multiple_of(c * CONV_ROWS, CONV_ROWS)
        win = ubuf[pl.ds(r0, CONV_ROWS + HALO), :]
        acc = jnp.broadcast_to(bias, (CONV_ROWS, d))
        for k in range(ksize):
            off = HALO - (ksize - 1) + k
            acc = acc + wdw_ref[k:k + 1, :] * win[off:off + CONV_ROWS, :]
        out_ref[pl.ds(r0, CONV_ROWS), :] = acc
        return 0

    lax.fori_loop(0, tm // CONV_ROWS, chunk, 0)


def _mixer_kernel(*refs, kind, tm, tiles_per_seq, with_router):
    if kind == "conformer":
        (x_ref, win_ref, wdw_ref, bdw_ref, cg_ref, cb_ref, wout_ref, g_ref, b_ref), refs = refs[:9], refs[9:]
    else:
        (x_ref, win_ref, wdw_ref, wout_ref, g_ref, b_ref), refs = refs[:6], refs[6:]
    if with_router:
        wr_ref, refs = refs[0], refs[1:]
        out_ref, idx_ref, rank_ref, wts_ref, cnt_ref = refs[:5]
        winb, woutb, ubuf, cbuf, carry = refs[5:]
    else:
        out_ref = refs[0]
        winb, woutb, ubuf, cbuf = refs[1:]
    i = pl.program_id(0)
    d = x_ref.shape[-1]

    @pl.when(i == 0)
    def _():
        winb[...] = win_ref[...].astype(BF16)
        woutb[...] = wout_ref[...].astype(BF16)
        if with_router:
            carry[...] = jnp.zeros_like(carry)

    @pl.when(i % tiles_per_seq == 0)
    def _():
        ubuf[0:HALO, :] = jnp.zeros((HALO, d), F32)

    x = x_ref[...]
    h = _mm(x.astype(BF16), winb[...])
    if kind == "conformer":
        ubuf[HALO:HALO + tm, :] = h[:, :d] * jax.nn.sigmoid(h[:, d:])
        _causal_conv(ubuf, wdw_ref, cbuf, bdw_ref[...], tm, CONV_KERNEL)
        v = jax.nn.silu(_layer_norm(cbuf[...], cg_ref[...], cb_ref[...]))
    else:
        ubuf[HALO:HALO + tm, :] = h[:, d:2 * d] * h[:, 2 * d:]
        _causal_conv(ubuf, wdw_ref, cbuf, jnp.zeros((1, d), F32), tm, SHORT_KERNEL)
        v = h[:, :d] * cbuf[...]
    ubuf[0:HALO, :] = ubuf[tm:tm + HALO, :]
    mix = _mm(v.astype(BF16), woutb[...])
    y = _layer_norm(ALPHA * x + mix, g_ref[...], b_ref[...])
    out_ref[...] = y
    if with_router:
        _route(y, wr_ref[...], carry, idx_ref, rank_ref, wts_ref, cnt_ref)


def _mixer(kind, x, weights, ln_g, ln_b, seq_len, w_router=None):
    n, d = x.shape
    tm = 512
    assert seq_len % tm == 0 and n % seq_len == 0
    with_router = w_router is not None
    if kind == "conformer":
        w_in, w_dw, b_dw, cg, cb, w_out = weights
        ins = [x, w_in, w_dw, b_dw[None], cg[None], cb[None], w_out, ln_g[None], ln_b[None]]
    else:
        w_in, w_dw, w_out = weights
        ins = [x, w_in, w_dw, w_out, ln_g[None], ln_b[None]]
    in_specs = [pl.BlockSpec((tm, d), lambda i: (i, 0))] + [_full(a.shape) for a in ins[1:]]
    out_shape = [jax.ShapeDtypeStruct((n, d), F32)]
    out_specs = [pl.BlockSpec((tm, d), lambda i: (i, 0))]
    scratch = [pltpu.VMEM(w_in.shape, BF16), pltpu.VMEM(w_out.shape, BF16),
               pltpu.VMEM((tm + HALO, d), F32), pltpu.VMEM((tm, d), F32)]
    if with_router:
        wr_t = w_router.T
        ins.append(wr_t)
        in_specs.append(_full(wr_t.shape))
        rs, rspec = _router_out(n, tm)
        out_shape += rs
        out_specs += rspec
        scratch.append(pltpu.VMEM((N_EXPERTS, 1), F32))
    return pl.pallas_call(
        functools.partial(_mixer_kernel, kind=kind, tm=tm, tiles_per_seq=seq_len // tm, with_router=with_router),
        grid=(n // tm,), in_specs=in_specs, out_specs=out_specs, out_shape=out_shape,
        scratch_shapes=scratch, compiler_params=_params(("arbitrary",)), name=f"mixer_{kind}",
    )(*ins)


def _ffn_kernel(x_ref, wg_ref, wu_ref, wd_ref, g_ref, b_ref, out_ref, xb, acc):
    j = pl.program_id(1)

    @pl.when(j == 0)
    def _():
        xb[...] = x_ref[...].astype(BF16)
        acc[...] = jnp.zeros_like(acc)

    hg = _mm(xb[...], wg_ref[...].astype(BF16))
    hu = _mm(xb[...], wu_ref[...].astype(BF16))
    acc[...] += _mm((jax.nn.silu(hg) * hu).astype(BF16), wd_ref[...].astype(BF16))

    @pl.when(j == pl.num_programs(1) - 1)
    def _():
        out_ref[...] = _layer_norm(ALPHA * x_ref[...] + acc[...], g_ref[...], b_ref[...])


def _dense_ffn(x, w_gate, w_up, w_down, ln_g, ln_b):
    n, d = x.shape
    dff = w_gate.shape[1]
    tm, tf = 1024, 256
    assert n % tm == 0 and dff % tf == 0
    return pl.pallas_call(
        _ffn_kernel, grid=(n // tm, dff // tf),
        in_specs=[pl.BlockSpec((tm, d), lambda i, j: (i, 0)),
                  pl.BlockSpec((d, tf), lambda i, j: (0, j)),
                  pl.BlockSpec((d, tf), lambda i, j: (0, j)),
                  pl.BlockSpec((tf, d), lambda i, j: (j, 0)),
                  pl.BlockSpec((1, d), lambda i, j: (0, 0)),
                  pl.BlockSpec((1, d), lambda i, j: (0, 0))],
        out_specs=pl.BlockSpec((tm, d), lambda i, j: (i, 0)),
        out_shape=jax.ShapeDtypeStruct((n, d), F32),
        scratch_shapes=[pltpu.VMEM((tm, d), BF16), pltpu.VMEM((tm, d), F32)],
        compiler_params=_params(("arbitrary", "arbitrary")), name="dense_ffn",
    )(x, w_gate, w_up, w_down, ln_g[None], ln_b[None])


MOE_TM = 1024
MOE_TF = 512
MOE_TB = 256


def _row_copy(src, dst, s, t, sem):
    return pltpu.make_async_copy(src.at[pl.ds(s, 1), :], dst.at[pl.ds(t, 1), :], sem)


def _dispatch_kernel(d0_ref, d1_ref, x_ref, zeros_hbm, xs_hbm, sem):
    del zeros_hbm
    base = pl.program_id(0) * MOE_TB

    def start(t, _):
        _row_copy(x_ref, xs_hbm, t, d0_ref[base + t], sem).start()
        _row_copy(x_ref, xs_hbm, t, d1_ref[base + t], sem).start()
        return 0

    def wait(t, _):
        _row_copy(x_ref, xs_hbm, t, d0_ref[base + t], sem).wait()
        _row_copy(x_ref, xs_hbm, t, d1_ref[base + t], sem).wait()
        return 0

    lax.fori_loop(0, MOE_TB, start, 0)
    lax.fori_loop(0, MOE_TB, wait, 0)


def _dispatch(x, dest0, dest1, rows):
    n, d = x.shape
    return pl.pallas_call(
        _dispatch_kernel,
        grid_spec=pltpu.PrefetchScalarGridSpec(
            num_scalar_prefetch=2, grid=(n // MOE_TB,),
            in_specs=[pl.BlockSpec((MOE_TB, d), lambda i, d0, d1: (i, 0)), pl.BlockSpec(memory_space=pl.ANY)],
            out_specs=pl.BlockSpec(memory_space=pl.ANY),
            scratch_shapes=[pltpu.SemaphoreType.DMA(())]),
        out_shape=jax.ShapeDtypeStruct((rows, d), F32),
        input_output_aliases={3: 0},
        compiler_params=_params(("arbitrary",)), name="moe_dispatch",
    )(dest0, dest1, x, jnp.zeros((rows, d), F32))


def _expert_kernel(te_ref, tx_ref, tv_ref, x_ref, wg_ref, wu_ref, wd_ref, out_ref, xb, acc):
    del te_ref, tx_ref
    i, j = pl.program_id(0), pl.program_id(1)

    @pl.when(tv_ref[i] == 1)
    def _():
        @pl.when(j == 0)
        def _():
            xb[...] = x_ref[...].astype(BF16)
            acc[...] = jnp.zeros_like(acc)

        hg = _mm(xb[...], wg_ref[0].astype(BF16))
        hu = _mm(xb[...], wu_ref[0].astype(BF16))
        acc[...] += _mm((jax.nn.silu(hg) * hu).astype(BF16), wd_ref[0].astype(BF16))

        @pl.when(j == pl.num_programs(1) - 1)
        def _():
            out_ref[...] = acc[...]

    @pl.when((tv_ref[i] == 0) & (j == pl.num_programs(1) - 1))
    def _():
        out_ref[...] = jnp.zeros_like(out_ref)


def _experts(xs, tile_expert, tile_row, tile_valid, w_gate, w_up, w_down):
    rows, d = xs.shape
    dff = w_gate.shape[2]
    n_ff = dff // MOE_TF
    assert dff % MOE_TF == 0 and rows % MOE_TM == 0

    def jj(i, j, tv):
        return jnp.where(tv[i] == 1, j, n_ff - 1)

    return pl.pallas_call(
        _expert_kernel,
        grid_spec=pltpu.PrefetchScalarGridSpec(
            num_scalar_prefetch=3, grid=(rows // MOE_TM, n_ff),
            in_specs=[pl.BlockSpec((MOE_TM, d), lambda i, j, te, tx, tv: (tx[i], 0)),
                      pl.BlockSpec((1, d, MOE_TF), lambda i, j, te, tx, tv: (te[i], 0, jj(i, j, tv))),
                      pl.BlockSpec((1, d, MOE_TF), lambda i, j, te, tx, tv: (te[i], 0, jj(i, j, tv))),
                      pl.BlockSpec((1, MOE_TF, d), lambda i, j, te, tx, tv: (te[i], jj(i, j, tv), 0))],
            out_specs=pl.BlockSpec((MOE_TM, d), lambda i, j, te, tx, tv: (i, 0)),
            scratch_shapes=[pltpu.VMEM((MOE_TM, d), BF16), pltpu.VMEM((MOE_TM, d), F32)]),
        out_shape=jax.ShapeDtypeStruct((rows, d), F32),
        compiler_params=_params(("arbitrary", "arbitrary")), name="moe_experts",
    )(tile_expert, tile_row, tile_valid, xs, w_gate, w_up, w_down)


def _combine_kernel(d0_ref, d1_ref, x_ref, w_ref, g_ref, b_ref, ys_hbm, out_ref, ybuf, sem):
    base = pl.program_id(0) * MOE_TB

    def start(t, _):
        _row_copy(ys_hbm, ybuf.at[0], d0_ref[base + t], t, sem).start()
        _row_copy(ys_hbm, ybuf.at[1], d1_ref[base + t], t, sem).start()
        return 0

    def wait(t, _):
        _row_copy(ys_hbm, ybuf.at[0], d0_ref[base + t], t, sem).wait()
        _row_copy(ys_hbm, ybuf.at[1], d1_ref[base + t], t, sem).wait()
        return 0

    lax.fori_loop(0, MOE_TB, start, 0)
    lax.fori_loop(0, MOE_TB, wait, 0)
    w = w_ref[...]
    ffn = w[:, 0:1] * ybuf[0] + w[:, 1:2] * ybuf[1]
    out_ref[...] = _layer_norm(ALPHA * x_ref[...] + ffn, g_ref[...], b_ref[...])


def _combine(x, ys, dest0, dest1, wts, ln_g, ln_b):
    n, d = x.shape
    return pl.pallas_call(
        _combine_kernel,
        grid_spec=pltpu.PrefetchScalarGridSpec(
            num_scalar_prefetch=2, grid=(n // MOE_TB,),
            in_specs=[pl.BlockSpec((MOE_TB, d), lambda i, d0, d1: (i, 0)),
                      pl.BlockSpec((MOE_TB, TOP_K), lambda i, d0, d1: (i, 0)),
                      pl.BlockSpec((1, d), lambda i, d0, d1: (0, 0)),
                      pl.BlockSpec((1, d), lambda i, d0, d1: (0, 0)),
                      pl.BlockSpec(memory_space=pl.ANY)],
            out_specs=pl.BlockSpec((MOE_TB, d), lambda i, d0, d1: (i, 0)),
            scratch_shapes=[pltpu.VMEM((TOP_K, MOE_TB, d), F32), pltpu.SemaphoreType.DMA(())]),
        out_shape=jax.ShapeDtypeStruct((n, d), F32),
        compiler_params=_params(("arbitrary",)), name="moe_combine",
    )(dest0, dest1, x, wts, ln_g[None], ln_b[None], ys)


def _moe(x, idx, rank, wts, counts, w_gate, w_up, w_down, ln_g, ln_b):
    n, _ = x.shape
    n_tiles = (TOP_K * n) // MOE_TM + N_EXPERTS
    rows = n_tiles * MOE_TM
    cnt = counts.astype(I32)
    padded = ((cnt + MOE_TM - 1) // MOE_TM) * MOE_TM
    ends = jnp.cumsum(padded)
    offsets = ends - padded
    expert_offset = jnp.sum(jnp.where(idx[..., None] == jnp.arange(N_EXPERTS, dtype=I32), offsets, 0), axis=-1)
    dest = expert_offset + rank
    tile_start = jnp.arange(n_tiles, dtype=I32) * MOE_TM
    tile_valid = (tile_start < ends[-1]).astype(I32)
    last = jnp.maximum(ends[-1] // MOE_TM - 1, 0)
    tile_row = jnp.minimum(jnp.arange(n_tiles, dtype=I32), last)
    tile_expert = jnp.minimum(jnp.sum(tile_row[:, None] * MOE_TM >= ends[None, :], axis=1), N_EXPERTS - 1).astype(I32)
    xs = _dispatch(x, dest[0], dest[1], rows)
    ys = _experts(xs, tile_expert, tile_row, tile_valid, w_gate, w_up, w_down)
    return _combine(x, ys, dest[0], dest[1], wts.T, ln_g, ln_b)


def _nsa_proj_kernel(x_ref, win_ref, q_ref, kv_ref, gate_ref, winb, *, qd, kvd, scale):
    @pl.when(pl.program_id(0) == 0)
    def _():
        winb[...] = win_ref[...].astype(BF16)

    h = _mm(x_ref[...].astype(BF16), winb[...])
    q_ref[...] = h[:, :qd] * scale
    kv_ref[...] = h[:, qd:qd + kvd]
    gate_ref[...] = jax.nn.sigmoid(h[:, qd + kvd:])


def _nsa_proj(x, w_in, qd, kvd):
    n, d = x.shape
    cols = w_in.shape[1]
    gd = cols - qd - kvd
    tm = 512
    dh = qd // NSA_HEADS
    return pl.pallas_call(
        functools.partial(_nsa_proj_kernel, qd=qd, kvd=kvd, scale=dh ** -0.5),
        grid=(n // tm,),
        in_specs=[pl.BlockSpec((tm, d), lambda i: (i, 0)), _full(w_in.shape)],
        out_specs=[pl.BlockSpec((tm, qd), lambda i: (i, 0)), pl.BlockSpec((tm, kvd), lambda i: (i, 0)),
                   pl.BlockSpec((tm, gd), lambda i: (i, 0))],
        out_shape=[jax.ShapeDtypeStruct((n, qd), F32), jax.ShapeDtypeStruct((n, kvd), F32),
                   jax.ShapeDtypeStruct((n, gd), F32)],
        scratch_shapes=[pltpu.VMEM(w_in.shape, BF16)],
        compiler_params=_params(("arbitrary",)), name="nsa_proj",
    )(x, w_in)


def _compress_kernel(ck_ref, cv_ref, pek_ref, pev_ref, wk1_ref, wk1c_ref, wk2_ref, wv1_ref, wv1c_ref, wv2_ref,
                     ko_ref, vo_ref):
    def one(c_ref, pe_ref, w1_ref, w1c_ref, w2_ref, o_ref):
        hid = w2_ref.shape[0]
        a = _mm(c_ref[0].astype(BF16), w1c_ref[...].astype(BF16))
        bias = _mm(pe_ref[...].astype(BF16), w1_ref[...].astype(BF16))[0:1, :]
        pre = a[:, :hid] + pltpu.roll(a[:, hid:], a.shape[0] - 1, 0) + bias
        o_ref[0] = _mm(jax.nn.gelu(pre).astype(BF16), w2_ref[...].astype(BF16))

    one(ck_ref, pek_ref, wk1_ref, wk1c_ref, wk2_ref, ko_ref)
    one(cv_ref, pev_ref, wv1_ref, wv1c_ref, wv2_ref, vo_ref)


def _compress(ck, cv, pe_k, pe_v, wk1, wk2, wv1, wv2):
    bh, chunks, cw = ck.shape
    dh = wk2.shape[1]

    def prep(pe, w1):
        pe_rows = jnp.zeros((SUBLANES, pe.size), F32).at[0].set(pe.reshape(-1))
        return pe_rows, jnp.concatenate([w1[:cw], w1[cw:]], axis=1)

    pek, wk1c = prep(pe_k, wk1)
    pev, wv1c = prep(pe_v, wv1)
    ins = [ck, cv, pek, pev, wk1, wk1c, wk2, wv1, wv1c, wv2]
    blk = pl.BlockSpec((1, chunks, cw), lambda i: (i, 0, 0))
    oblk = pl.BlockSpec((1, chunks, dh), lambda i: (i, 0, 0))
    return pl.pallas_call(
        _compress_kernel, grid=(bh,),
        in_specs=[blk, blk] + [_full(a.shape) for a in ins[2:]],
        out_specs=[oblk, oblk],
        out_shape=[jax.ShapeDtypeStruct((bh, chunks, dh), F32)] * 2,
        compiler_params=_params(("arbitrary",)), name="nsa_compress",
    )(*ins)


NSA_TQ = 256
NSA_TK = 256


def _masked_softmax(s, mask):
    s = jnp.where(mask, s, NEG_INF)
    m = jnp.max(s, -1, keepdims=True)
    e = jnp.exp(s - m) * mask.astype(F32)
    return e / jnp.maximum(jnp.sum(e, -1, keepdims=True), 1e-30)


def _attn_kernel(q_ref, gate_ref, kc_ref, vc_ref, ks_ref, vs_ref, kw_ref, vw_ref, o_ref, m_scr, l_scr, acc_scr):
    g, tq, dh = q_ref.shape[1:]
    m_rows = g * tq
    n_cmp_pad = kc_ref.shape[1]
    n_slc = ks_ref.shape[1] // SLC_BLOCK
    qi = pl.program_id(1)
    t0 = qi * tq
    qb = q_ref[0].reshape(m_rows, dh).astype(BF16)
    t_col = t0 + lax.broadcasted_iota(I32, (tq, 1), 0)
    t_rows = jnp.concatenate([t_col] * g, axis=0)

    c_idx = lax.broadcasted_iota(I32, (tq, n_cmp_pad), 1)
    mask_c = (c_idx * CMP_STRIDE + CMP_BLOCK - 1 <= t_col) & (c_idx < n_cmp_pad - 1)
    s_c = _nt(qb, kc_ref[0].astype(BF16)).reshape(g, tq, n_cmp_pad)
    p_c = _masked_softmax(s_c, jnp.broadcast_to(mask_c[None], s_c.shape))
    o_cmp = _mm(p_c.reshape(m_rows, n_cmp_pad).astype(BF16), vc_ref[0].astype(BF16))

    p_sum = jnp.sum(p_c, axis=0)
    n_i = lax.broadcasted_iota(I32, (n_slc, n_cmp_pad), 0) * SLC_BLOCK
    c_s = lax.broadcasted_iota(I32, (n_slc, n_cmp_pad), 1) * CMP_STRIDE
    overlap_t = ((c_s < n_i + SLC_BLOCK) & (c_s + CMP_BLOCK > n_i)).astype(F32)
    imp_t = _nt(overlap_t, p_sum, precision=lax.Precision.HIGHEST)
    blk = lax.broadcasted_iota(I32, (n_slc, tq), 0)
    t_lane = t0 + lax.broadcasted_iota(I32, (n_slc, tq), 1)
    d_cur = t_lane // SLC_BLOCK - blk
    forced = (blk == 0) | ((d_cur >= 0) & (d_cur < SLC_FORCED_LOCAL))
    score = jnp.where(forced, FORCE_BONUS, jnp.where(blk * SLC_BLOCK <= t_lane, imp_t, NEG_INF))
    rank = jnp.zeros((n_slc, tq), I32)
    for m in range(n_slc):
        sm = score[m:m + 1, :]
        rank = rank + ((sm > score) | ((sm == score) & (blk > m))).astype(I32)
    pen_t = jnp.where(rank < min(SLC_TOP_N, n_slc), 0.0, SEL_PENALTY).astype(BF16)
    eye = (lax.broadcasted_iota(I32, (tq, tq), 0) == lax.broadcasted_iota(I32, (tq, tq), 1)).astype(BF16)
    pen = _nt(eye, pen_t).astype(BF16)
    pen_rows = jnp.concatenate([pen] * g, axis=0)

    m_scr[...] = jnp.full(m_scr.shape, NEG_INF, F32)
    l_scr[...] = jnp.zeros_like(l_scr)
    acc_scr[...] = jnp.zeros_like(acc_scr)

    def sel_step(kt, _):
        k0 = pl.multiple_of(kt * NSA_TK, NSA_TK)
        kb = ks_ref[0, pl.ds(k0, NSA_TK), :].astype(BF16)
        vb = vs_ref[0, pl.ds(k0, NSA_TK), :].astype(BF16)
        j_pos = k0 + lax.broadcasted_iota(I32, (n_slc, NSA_TK), 1)
        expand = (j_pos // SLC_BLOCK == lax.broadcasted_iota(I32, (n_slc, NSA_TK), 0)).astype(BF16)
        s = _nt(qb, kb) + _mm(pen_rows, expand)
        key_pos = k0 + lax.broadcasted_iota(I32, (m_rows, NSA_TK), 1)
        s = jnp.where(key_pos <= t_rows, s, NEG_INF)
        m_old = m_scr[...]
        m_new = jnp.maximum(m_old, jnp.max(s, -1, keepdims=True))
        a = jnp.exp(m_old - m_new)
        p = jnp.exp(s - m_new)
        l_scr[...] = a * l_scr[...] + jnp.sum(p, -1, keepdims=True)
        acc_scr[...] = a * acc_scr[...] + _mm(p.astype(BF16), vb)
        m_scr[...] = m_new
        return 0

    lax.fori_loop(0, (t0 + tq) // NSA_TK, sel_step, 0)
    o_sel = acc_scr[...] / l_scr[...]

    span = WINDOW + tq
    kwin = kw_ref[0, pl.ds(pl.multiple_of(t0, tq), span), :].astype(BF16)
    vwin = vw_ref[0, pl.ds(pl.multiple_of(t0, tq), span), :].astype(BF16)
    kpos = t0 - WINDOW + lax.broadcasted_iota(I32, (m_rows, span), 1)
    diff = t_rows - kpos
    p_w = _masked_softmax(_nt(qb, kwin), (diff >= 0) & (diff < WINDOW) & (kpos >= 0))
    o_win = _mm(p_w.astype(BF16), vwin)

    gate = gate_ref[0].reshape(m_rows, 3)
    o = gate[:, 0:1] * o_cmp + gate[:, 1:2] * o_sel + gate[:, 2:3] * o_win
    o_ref[0] = o.reshape(g, tq, dh)


def _attention(q5, gates, k_cmp, v_cmp, ks, vs, kw_pad, vw_pad):
    bh, g, t, dh = q5.shape
    tq = NSA_TQ
    n_cmp_pad = k_cmp.shape[1]
    m_rows = g * tq
    qblk = pl.BlockSpec((1, g, tq, dh), lambda b, i: (b, 0, i, 0))
    cblk = pl.BlockSpec((1, n_cmp_pad, dh), lambda b, i: (b, 0, 0))
    kblk = pl.BlockSpec((1, t, dh), lambda b, i: (b, 0, 0))
    wblk = pl.BlockSpec((1, t + WINDOW, dh), lambda b, i: (b, 0, 0))
    return pl.pallas_call(
        _attn_kernel, grid=(bh, t // tq),
        in_specs=[qblk, pl.BlockSpec((1, g, tq, 3), lambda b, i: (b, 0, i, 0)), cblk, cblk, kblk, kblk, wblk, wblk],
        out_specs=qblk,
        out_shape=jax.ShapeDtypeStruct((bh, g, t, dh), F32),
        scratch_shapes=[pltpu.VMEM((m_rows, 1), F32), pltpu.VMEM((m_rows, 1), F32), pltpu.VMEM((m_rows, dh), F32)],
        compiler_params=_params(("arbitrary", "arbitrary")), name="nsa_attention",
    )(q5, gates, k_cmp, v_cmp, ks, vs, kw_pad, vw_pad)


def _outproj_kernel(o_ref, x_ref, w_ref, g_ref, b_ref, out_ref, wb):
    @pl.when(pl.program_id(0) == 0)
    def _():
        wb[...] = w_ref[...].astype(BF16)

    mix = _mm(o_ref[...].astype(BF16), wb[...])
    out_ref[...] = _layer_norm(ALPHA * x_ref[...] + mix, g_ref[...], b_ref[...])


def _outproj(o, x, w_out, ln_g, ln_b):
    n, d = x.shape
    tm = 512
    blk = pl.BlockSpec((tm, d), lambda i: (i, 0))
    return pl.pallas_call(
        _outproj_kernel, grid=(n // tm,),
        in_specs=[blk, blk, _full(w_out.shape), _full((1, d)), _full((1, d))],
        out_specs=blk, out_shape=jax.ShapeDtypeStruct((n, d), F32),
        scratch_shapes=[pltpu.VMEM(w_out.shape, BF16)],
        compiler_params=_params(("arbitrary",)), name="nsa_outproj",
    )(o, x, w_out, ln_g[None], ln_b[None])


def _nsa_mixer(x, batch, seq, w_in, pe_k, pe_v, wk1, wk2, wv1, wv2, w_out, ln_g, ln_b):
    n, d = x.shape
    h, hk, g = NSA_HEADS, NSA_KV_HEADS, NSA_GROUP
    dh = d // h
    qd, kd = h * dh, hk * dh
    q, kv, gates = _nsa_proj(x, w_in, qd, 6 * kd)
    q5 = q.reshape(batch, seq, hk, g, dh).transpose(0, 2, 3, 1, 4).reshape(batch * hk, g, seq, dh)
    gates = gates.reshape(batch, seq, hk, g, 3).transpose(0, 2, 3, 1, 4).reshape(batch * hk, g, seq, 3)
    kv = kv.reshape(batch, seq, 6, hk, dh).transpose(2, 0, 3, 1, 4).reshape(6, batch * hk, seq, dh)
    kc, vc, ks, vs, kw, vw = (kv[i] for i in range(6))
    chunks = seq // CMP_STRIDE
    k_cmp, v_cmp = _compress(kc.reshape(batch * hk, chunks, CMP_STRIDE * dh), vc.reshape(batch * hk, chunks, CMP_STRIDE * dh),
                             pe_k, pe_v, wk1, wk2, wv1, wv2)
    pad = ((0, 0), (WINDOW, 0), (0, 0))
    o5 = _attention(q5, gates, k_cmp, v_cmp, ks, vs, jnp.pad(kw, pad), jnp.pad(vw, pad))
    o = o5.reshape(batch, hk, g, seq, dh).transpose(0, 3, 1, 2, 4).reshape(n, d)
    return _outproj(o, x, w_out, ln_g, ln_b)


def kernel(x, ln_mix_g, ln_mix_b, ln_ffn_g, ln_ffn_b, conf_w_in, conf_w_dw, conf_b_dw, conf_ln_g, conf_ln_b, conf_w_out, sc_w_in, sc_w_conv, sc_w_out, nsa_w_in, nsa_pe_k, nsa_pe_v, nsa_wk1, nsa_wk2, nsa_wv1, nsa_wv2, nsa_w_out, ffn_w_gate, ffn_w_up, ffn_w_down, moe_w_router, moe_w_gate, moe_w_up, moe_w_down):
    batch, seq, d = x.shape
    h = x.reshape(batch * seq, d)
    for i in range(DEPTH):
        kind, j = i % N_MIXERS, i // N_MIXERS
        f = i // 2
        w_router = moe_w_router[f] if i % 2 == 1 else None
        if kind == 0:
            res = _mixer("conformer", h, (conf_w_in[j], conf_w_dw[j], conf_b_dw[j], conf_ln_g[j], conf_ln_b[j], conf_w_out[j]),
                         ln_mix_g[i], ln_mix_b[i], seq, w_router)
        elif kind == 1:
            res = _mixer("shortconv", h, (sc_w_in[j], sc_w_conv[j], sc_w_out[j]), ln_mix_g[i], ln_mix_b[i], seq, w_router)
        else:
            assert w_router is None
            res = [_nsa_mixer(h, batch, seq, nsa_w_in[j], nsa_pe_k[j], nsa_pe_v[j], nsa_wk1[j], nsa_wk2[j],
                              nsa_wv1[j], nsa_wv2[j], nsa_w_out[j], ln_mix_g[i], ln_mix_b[i])]
        h = res[0]
        if i % 2 == 0:
            h = _dense_ffn(h, ffn_w_gate[f], ffn_w_up[f], ffn_w_down[f], ln_ffn_g[i], ln_ffn_b[i])
        else:
            _, idx, rank, wts, cnt = res
            h = _moe(h, idx, rank, wts, cnt[:, 0], moe_w_gate[f], moe_w_up[f], moe_w_down[f], ln_ffn_g[i], ln_ffn_b[i])
    return h.reshape(batch, seq, d)
```

```python
import functools

import jax
import jax.numpy as jnp
from jax import lax
from jax.experimental import pallas as pl
from jax.experimental.pallas import tpu as pltpu

F32 = jnp.float32
BF16 = jnp.bfloat16
I32 = jnp.int32

DEPTH = 4
N_MIXERS = 3
ALPHA = (2.0 * DEPTH) ** 0.25
LN_EPS = 1e-5
NEG_INF = -1e30

CONV_KERNEL = 31
SHORT_KERNEL = 3
NSA_HEADS = 16
NSA_KV_HEADS = 4
NSA_GROUP = NSA_HEADS // NSA_KV_HEADS
CMP_BLOCK = 32
CMP_STRIDE = 16
SLC_BLOCK = 64
SLC_TOP_N = 16
SLC_FORCED_LOCAL = 2
FORCE_BONUS = 1e4
WINDOW = 512
N_EXPERTS = 8
TOP_K = 2

LANES = 128
SUBLANES = 8
VMEM_LIMIT_BYTES = 56 * 1024 * 1024

HALO = 32
CONV_ROWS = 16
SEL_PENALTY = -1e9


def _nt(a, b, **kw):
    return lax.dot_general(a, b, (((1,), (1,)), ((), ())), preferred_element_type=F32, **kw)


def _mm(a, b):
    return jnp.dot(a, b, preferred_element_type=F32)


def _layer_norm(v, g, b):
    mu = jnp.mean(v, -1, keepdims=True)
    d = v - mu
    var = jnp.mean(d * d, -1, keepdims=True)
    return d * lax.rsqrt(var + LN_EPS) * g + b


def _full(shape):
    return pl.BlockSpec(shape, lambda *_: (0,) * len(shape), pipeline_mode=pl.Buffered(1))


def _params(semantics):
    return pltpu.CompilerParams(dimension_semantics=semantics, vmem_limit_bytes=VMEM_LIMIT_BYTES)


def _route(y, wr_t, carry_ref, idx_ref, rank_ref, wts_ref, cnt_ref):
    tm = y.shape[0]
    logits = _nt(wr_t, y, precision=lax.Precision.HIGHEST)
    eidx = lax.broadcasted_iota(I32, logits.shape, 0).astype(F32)
    m1 = jnp.max(logits, axis=0, keepdims=True)
    i1 = jnp.min(jnp.where(logits == m1, eidx, float(N_EXPERTS)), axis=0, keepdims=True)
    rest = jnp.where(eidx == i1, -jnp.inf, logits)
    m2 = jnp.max(rest, axis=0, keepdims=True)
    i2 = jnp.min(jnp.where(rest == m2, eidx, float(N_EXPERTS)), axis=0, keepdims=True)
    e2 = jnp.exp(m2 - m1)
    denom = 1.0 + e2
    chosen = (eidx == i1) | (eidx == i2)
    chosen_f = chosen.astype(F32)
    upper = (lax.broadcasted_iota(I32, (tm, tm), 0) < lax.broadcasted_iota(I32, (tm, tm), 1)).astype(BF16)
    before = _mm(chosen_f.astype(BF16), upper) + carry_ref[...]
    r1 = jnp.sum(jnp.where(eidx == i1, before, 0.0), axis=0, keepdims=True)
    r2 = jnp.sum(jnp.where(eidx == i2, before, 0.0), axis=0, keepdims=True)
    carry_ref[...] = carry_ref[...] + jnp.sum(chosen_f, axis=1, keepdims=True)
    idx_ref[...] = jnp.concatenate([i1, i2], axis=0).astype(I32)
    rank_ref[...] = jnp.concatenate([r1, r2], axis=0).astype(I32)
    wts_ref[...] = jnp.concatenate([1.0 / denom, e2 / denom], axis=0)
    cnt_ref[...] = jnp.broadcast_to(carry_ref[...], cnt_ref.shape)


def _router_out(n, tm):
    shapes = [jax.ShapeDtypeStruct((TOP_K, n), I32), jax.ShapeDtypeStruct((TOP_K, n), I32),
              jax.ShapeDtypeStruct((TOP_K, n), F32), jax.ShapeDtypeStruct((N_EXPERTS, LANES), F32)]
    specs = [pl.BlockSpec((TOP_K, tm), lambda i: (0, i))] * 3 + [pl.BlockSpec((N_EXPERTS, LANES), lambda i: (0, 0))]
    return shapes, specs


def _causal_conv(ubuf, wdw_ref, out_ref, bias, tm, ksize):
    d = out_ref.shape[-1]

    def chunk(c, _):
        r0 = pl.multiple_of(c * CONV_ROWS, CONV_ROWS)
        win = ubuf[pl.ds(r0, CONV_ROWS + HALO), :]
        acc = jnp.broadcast_to(bias, (CONV_ROWS, d))
        for k in range(ksize):
            off = HALO - (ksize - 1) + k
            acc = acc + wdw_ref[k:k + 1, :] * win[off:off + CONV_ROWS, :]
        out_ref[pl.ds(r0, CONV_ROWS), :] = acc
        return 0

    lax.fori_loop(0, tm // CONV_ROWS, chunk, 0)


def _mixer_kernel(*refs, kind, tm, tiles_per_seq, with_router):
    if kind == "conformer":
        (x_ref, win_ref, wdw_ref, bdw_ref, cg_ref, cb_ref, wout_ref, g_ref, b_ref), refs = refs[:9], refs[9:]
    else:
        (x_ref, win_ref, wdw_ref, wout_ref, g_ref, b_ref), refs = refs[:6], refs[6:]
    if with_router:
        wr_ref, refs = refs[0], refs[1:]
        out_ref, idx_ref, rank_ref, wts_ref, cnt_ref = refs[:5]
        winb, woutb, ubuf, cbuf, carry = refs[5:]
    else:
        out_ref = refs[0]
        winb, woutb, ubuf, cbuf = refs[1:]
    i = pl.program_id(0)
    d = x_ref.shape[-1]

    @pl.when(i == 0)
    def _():
        winb[...] = win_ref[...].astype(BF16)
        woutb[...] = wout_ref[...].astype(BF16)
        if with_router:
            carry[...] = jnp.zeros_like(carry)

    @pl.when(i % tiles_per_seq == 0)
    def _():
        ubuf[0:HALO, :] = jnp.zeros((HALO, d), F32)

    x = x_ref[...]
    h = _mm(x.astype(BF16), winb[...])
    if kind == "conformer":
        ubuf[HALO:HALO + tm, :] = h[:, :d] * jax.nn.sigmoid(h[:, d:])
        _causal_conv(ubuf, wdw_ref, cbuf, bdw_ref[...], tm, CONV_KERNEL)
        v = jax.nn.silu(_layer_norm(cbuf[...], cg_ref[...], cb_ref[...]))
    else:
        ubuf[HALO:HALO + tm, :] = h[:, d:2 * d] * h[:, 2 * d:]
        _causal_conv(ubuf, wdw_ref, cbuf, jnp.zeros((1, d), F32), tm, SHORT_KERNEL)
        v = h[:, :d] * cbuf[...]
    ubuf[0:HALO, :] = ubuf[tm:tm + HALO, :]
    mix = _mm(v.astype(BF16), woutb[...])
    y = _layer_norm(ALPHA * x + mix, g_ref[...], b_ref[...])
    out_ref[...] = y
    if with_router:
        _route(y, wr_ref[...], carry, idx_ref, rank_ref, wts_ref, cnt_ref)


def _mixer(kind, x, weights, ln_g, ln_b, seq_len, w_router=None):
    n, d = x.shape
    tm = 512
    assert seq_len % tm == 0 and n % seq_len == 0
    with_router = w_router is not None
    if kind == "conformer":
        w_in, w_dw, b_dw, cg, cb, w_out = weights
        ins = [x, w_in, w_dw, b_dw[None], cg[None], cb[None], w_out, ln_g[None], ln_b[None]]
    else:
        w_in, w_dw, w_out = weights
        ins = [x, w_in, w_dw, w_out, ln_g[None], ln_b[None]]
    in_specs = [pl.BlockSpec((tm, d), lambda i: (i, 0))] + [_full(a.shape) for a in ins[1:]]
    out_shape = [jax.ShapeDtypeStruct((n, d), F32)]
    out_specs = [pl.BlockSpec((tm, d), lambda i: (i, 0))]
    scratch = [pltpu.VMEM(w_in.shape, BF16), pltpu.VMEM(w_out.shape, BF16),
               pltpu.VMEM((tm + HALO, d), F32), pltpu.VMEM((tm, d), F32)]
    if with_router:
        wr_t = w_router.T
        ins.append(wr_t)
        in_specs.append(_full(wr_t.shape))
        rs, rspec = _router_out(n, tm)
        out_shape += rs
        out_specs += rspec
        scratch.append(pltpu.VMEM((N_EXPERTS, 1), F32))
    return pl.pallas_call(
        functools.partial(_mixer_kernel, kind=kind, tm=tm, tiles_per_seq=seq_len // tm, with_router=with_router),
        grid=(n // tm,), in_specs=in_specs, out_specs=out_specs, out_shape=out_shape,
        scratch_shapes=scratch, compiler_params=_params(("arbitrary",)), name=f"mixer_{kind}",
    )(*ins)


def _ffn_kernel(x_ref, wg_ref, wu_ref, wd_ref, g_ref, b_ref, out_ref, xb, acc):
    j = pl.program_id(1)

    @pl.when(j == 0)
    def _():
        xb[...] = x_ref[...].astype(BF16)
        acc[...] = jnp.zeros_like(acc)

    hg = _mm(xb[...], wg_ref[...].astype(BF16))
    hu = _mm(xb[...], wu_ref[...].astype(BF16))
    acc[...] += _mm((jax.nn.silu(hg) * hu).astype(BF16), wd_ref[...].astype(BF16))

    @pl.when(j == pl.num_programs(1) - 1)
    def _():
        out_ref[...] = _layer_norm(ALPHA * x_ref[...] + acc[...], g_ref[...], b_ref[...])


def _dense_ffn(x, w_gate, w_up, w_down, ln_g, ln_b):
    n, d = x.shape
    dff = w_gate.shape[1]
    tm, tf = 1024, 256
    assert n % tm == 0 and dff % tf == 0
    return pl.pallas_call(
        _ffn_kernel, grid=(n // tm, dff // tf),
        in_specs=[pl.BlockSpec((tm, d), lambda i, j: (i, 0)),
                  pl.BlockSpec((d, tf), lambda i, j: (0, j)),
                  pl.BlockSpec((d, tf), lambda i, j: (0, j)),
                  pl.BlockSpec((tf, d), lambda i, j: (j, 0)),
                  pl.BlockSpec((1, d), lambda i, j: (0, 0)),
                  pl.BlockSpec((1, d), lambda i, j: (0, 0))],
        out_specs=pl.BlockSpec((tm, d), lambda i, j: (i, 0)),
        out_shape=jax.ShapeDtypeStruct((n, d), F32),
        scratch_shapes=[pltpu.VMEM((tm, d), BF16), pltpu.VMEM((tm, d), F32)],
        compiler_params=_params(("arbitrary", "arbitrary")), name="dense_ffn",
    )(x, w_gate, w_up, w_down, ln_g[None], ln_b[None])


MOE_TM = 1024
MOE_TF = 512
MOE_TB = 256


def _row_copy(src, dst, s, t, sem):
    return pltpu.make_async_copy(src.at[pl.ds(s, 1), :], dst.at[pl.ds(t, 1), :], sem)


def _dispatch_kernel(d0_ref, d1_ref, x_ref, zeros_hbm, xs_hbm, sem):
    del zeros_hbm
    base = pl.program_id(0) * MOE_TB

    def start(t, _):
        _row_copy(x_ref, xs_hbm, t, d0_ref[base + t], sem).start()
        _row_copy(x_ref, xs_hbm, t, d1_ref[base + t], sem).start()
        return 0

    def wait(t, _):
        _row_copy(x_ref, xs_hbm, t, d0_ref[base + t], sem).wait()
        _row_copy(x_ref, xs_hbm, t, d1_ref[base + t], sem).wait()
        return 0

    lax.fori_loop(0, MOE_TB, start, 0, unroll=8)
    lax.fori_loop(0, MOE_TB, wait, 0, unroll=8)


def _dispatch(x, dest0, dest1, rows):
    n, d = x.shape
    return pl.pallas_call(
        _dispatch_kernel,
        grid_spec=pltpu.PrefetchScalarGridSpec(
            num_scalar_prefetch=2, grid=(n // MOE_TB,),
            in_specs=[pl.BlockSpec((MOE_TB, d), lambda i, d0, d1: (i, 0)), pl.BlockSpec(memory_space=pl.ANY)],
            out_specs=pl.BlockSpec(memory_space=pl.ANY),
            scratch_shapes=[pltpu.SemaphoreType.DMA(())]),
        out_shape=jax.ShapeDtypeStruct((rows, d), F32),
        input_output_aliases={3: 0},
        compiler_params=_params(("arbitrary",)), name="moe_dispatch",
    )(dest0, dest1, x, jnp.zeros((rows, d), F32))


def _expert_kernel(te_ref, tx_ref, tv_ref, x_ref, wg_ref, wu_ref, wd_ref, out_ref, xb, acc):
    del te_ref, tx_ref
    i, j = pl.program_id(0), pl.program_id(1)

    @pl.when(tv_ref[i] == 1)
    def _():
        @pl.when(j == 0)
        def _():
            xb[...] = x_ref[...].astype(BF16)
            acc[...] = jnp.zeros_like(acc)

        hg = _mm(xb[...], wg_ref[0].astype(BF16))
        hu = _mm(xb[...], wu_ref[0].astype(BF16))
        acc[...] += _mm((jax.nn.silu(hg) * hu).astype(BF16), wd_ref[0].astype(BF16))

        @pl.when(j == pl.num_programs(1) - 1)
        def _():
            out_ref[...] = acc[...]

    @pl.when((tv_ref[i] == 0) & (j == pl.num_programs(1) - 1))
    def _():
        out_ref[...] = jnp.zeros_like(out_ref)


def _experts(xs, tile_expert, tile_row, tile_valid, w_gate, w_up, w_down):
    rows, d = xs.shape
    dff = w_gate.shape[2]
    n_ff = dff // MOE_TF
    assert dff % MOE_TF == 0 and rows % MOE_TM == 0

    def jj(i, j, tv):
        return jnp.where(tv[i] == 1, j, n_ff - 1)

    return pl.pallas_call(
        _expert_kernel,
        grid_spec=pltpu.PrefetchScalarGridSpec(
            num_scalar_prefetch=3, grid=(rows // MOE_TM, n_ff),
            in_specs=[pl.BlockSpec((MOE_TM, d), lambda i, j, te, tx, tv: (tx[i], 0)),
                      pl.BlockSpec((1, d, MOE_TF), lambda i, j, te, tx, tv: (te[i], 0, jj(i, j, tv))),
                      pl.BlockSpec((1, d, MOE_TF), lambda i, j, te, tx, tv: (te[i], 0, jj(i, j, tv))),
                      pl.BlockSpec((1, MOE_TF, d), lambda i, j, te, tx, tv: (te[i], jj(i, j, tv), 0))],
            out_specs=pl.BlockSpec((MOE_TM, d), lambda i, j, te, tx, tv: (i, 0)),
            scratch_shapes=[pltpu.VMEM((MOE_TM, d), BF16), pltpu.VMEM((MOE_TM, d), F32)]),
        out_shape=jax.ShapeDtypeStruct((rows, d), F32),
        compiler_params=_params(("arbitrary", "arbitrary")), name="moe_experts",
    )(tile_expert, tile_row, tile_valid, xs, w_gate, w_up, w_down)


def _combine_kernel(d0_ref, d1_ref, x_ref, w_ref, g_ref, b_ref, ys_hbm, out_ref, ybuf, sem):
    base = pl.program_id(0) * MOE_TB

    def start(t, _):
        _row_copy(ys_hbm, ybuf.at[0], d0_ref[base + t], t, sem).start()
        _row_copy(ys_hbm, ybuf.at[1], d1_ref[base + t], t, sem).start()
        return 0

    def wait(t, _):
        _row_copy(ys_hbm, ybuf.at[0], d0_ref[base + t], t, sem).wait()
        _row_copy(ys_hbm, ybuf.at[1], d1_ref[base + t], t, sem).wait()
        return 0

    lax.fori_loop(0, MOE_TB, start, 0, unroll=8)
    lax.fori_loop(0, MOE_TB, wait, 0, unroll=8)
    w = w_ref[...]
    ffn = w[:, 0:1] * ybuf[0] + w[:, 1:2] * ybuf[1]
    out_ref[...] = _layer_norm(ALPHA * x_ref[...] + ffn, g_ref[...], b_ref[...])


def _combine(x, ys, dest0, dest1, wts, ln_g, ln_b):
    n, d = x.shape
    return pl.pallas_call(
        _combine_kernel,
        grid_spec=pltpu.PrefetchScalarGridSpec(
            num_scalar_prefetch=2, grid=(n // MOE_TB,),
            in_specs=[pl.BlockSpec((MOE_TB, d), lambda i, d0, d1: (i, 0)),
                      pl.BlockSpec((MOE_TB, TOP_K), lambda i, d0, d1: (i, 0)),
                      pl.BlockSpec((1, d), lambda i, d0, d1: (0, 0)),
                      pl.BlockSpec((1, d), lambda i, d0, d1: (0, 0)),
                      pl.BlockSpec(memory_space=pl.ANY)],
            out_specs=pl.BlockSpec((MOE_TB, d), lambda i, d0, d1: (i, 0)),
            scratch_shapes=[pltpu.VMEM((TOP_K, MOE_TB, d), F32), pltpu.SemaphoreType.DMA(())]),
        out_shape=jax.ShapeDtypeStruct((n, d), F32),
        compiler_params=_params(("arbitrary",)), name="moe_combine",
    )(dest0, dest1, x, wts, ln_g[None], ln_b[None], ys)


def _moe(x, idx, rank, wts, counts, w_gate, w_up, w_down, ln_g, ln_b):
    n, _ = x.shape
    n_tiles = (TOP_K * n) // MOE_TM + N_EXPERTS
    rows = n_tiles * MOE_TM
    cnt = counts.astype(I32)
    padded = ((cnt + MOE_TM - 1) // MOE_TM) * MOE_TM
    ends = jnp.cumsum(padded)
    offsets = ends - padded
    expert_offset = jnp.sum(jnp.where(idx[..., None] == jnp.arange(N_EXPERTS, dtype=I32), offsets, 0), axis=-1)
    dest = expert_offset + rank
    tile_start = jnp.arange(n_tiles, dtype=I32) * MOE_TM
    tile_valid = (tile_start < ends[-1]).astype(I32)
    last = jnp.maximum(ends[-1] // MOE_TM - 1, 0)
    tile_row = jnp.minimum(jnp.arange(n_tiles, dtype=I32), last)
    tile_expert = jnp.minimum(jnp.sum(tile_row[:, None] * MOE_TM >= ends[None, :], axis=1), N_EXPERTS - 1).astype(I32)
    xs = _dispatch(x, dest[0], dest[1], rows)
    ys = _experts(xs, tile_expert, tile_row, tile_valid, w_gate, w_up, w_down)
    return _combine(x, ys, dest[0], dest[1], wts.T, ln_g, ln_b)


def _nsa_proj_kernel(x_ref, win_ref, q_ref, kv_ref, gate_ref, winb, *, qd, kvd, scale):
    @pl.when(pl.program_id(0) == 0)
    def _():
        winb[...] = win_ref[...].astype(BF16)

    h = _mm(x_ref[...].astype(BF16), winb[...])
    q_ref[...] = h[:, :qd] * scale
    kv_ref[...] = h[:, qd:qd + kvd]
    gate_ref[...] = jax.nn.sigmoid(h[:, qd + kvd:])


def _nsa_proj(x, w_in, qd, kvd):
    n, d = x.shape
    cols = w_in.shape[1]
    gd = cols - qd - kvd
    tm = 512
    dh = qd // NSA_HEADS
    return pl.pallas_call(
        functools.partial(_nsa_proj_kernel, qd=qd, kvd=kvd, scale=dh ** -0.5),
        grid=(n // tm,),
        in_specs=[pl.BlockSpec((tm, d), lambda i: (i, 0)), _full(w_in.shape)],
        out_specs=[pl.BlockSpec((tm, qd), lambda i: (i, 0)), pl.BlockSpec((tm, kvd), lambda i: (i, 0)),
                   pl.BlockSpec((tm, gd), lambda i: (i, 0))],
        out_shape=[jax.ShapeDtypeStruct((n, qd), F32), jax.ShapeDtypeStruct((n, kvd), F32),
                   jax.ShapeDtypeStruct((n, gd), F32)],
        scratch_shapes=[pltpu.VMEM(w_in.shape, BF16)],
        compiler_params=_params(("arbitrary",)), name="nsa_proj",
    )(x, w_in)


def _compress_kernel(ck_ref, cv_ref, pek_ref, pev_ref, wk1_ref, wk1c_ref, wk2_ref, wv1_ref, wv1c_ref, wv2_ref,
                     ko_ref, vo_ref):
    def one(c_ref, pe_ref, w1_ref, w1c_ref, w2_ref, o_ref):
        hid = w2_ref.shape[0]
        a = _mm(c_ref[0].astype(BF16), w1c_ref[...].astype(BF16))
        bias = _mm(pe_ref[...].astype(BF16), w1_ref[...].astype(BF16))[0:1, :]
        pre = a[:, :hid] + pltpu.roll(a[:, hid:], a.shape[0] - 1, 0) + bias
        o_ref[0] = _mm(jax.nn.gelu(pre).astype(BF16), w2_ref[...].astype(BF16))

    one(ck_ref, pek_ref, wk1_ref, wk1c_ref, wk2_ref, ko_ref)
    one(cv_ref, pev_ref, wv1_ref, wv1c_ref, wv2_ref, vo_ref)


def _compress(ck, cv, pe_k, pe_v, wk1, wk2, wv1, wv2):
    bh, chunks, cw = ck.shape
    dh = wk2.shape[1]

    def prep(pe, w1):
        pe_rows = jnp.zeros((SUBLANES, pe.size), F32).at[0].set(pe.reshape(-1))
        return pe_rows, jnp.concatenate([w1[:cw], w1[cw:]], axis=1)

    pek, wk1c = prep(pe_k, wk1)
    pev, wv1c = prep(pe_v, wv1)
    ins = [ck, cv, pek, pev, wk1, wk1c, wk2, wv1, wv1c, wv2]
    blk = pl.BlockSpec((1, chunks, cw), lambda i: (i, 0, 0))
    oblk = pl.BlockSpec((1, chunks, dh), lambda i: (i, 0, 0))
    return pl.pallas_call(
        _compress_kernel, grid=(bh,),
        in_specs=[blk, blk] + [_full(a.shape) for a in ins[2:]],
        out_specs=[oblk, oblk],
        out_shape=[jax.ShapeDtypeStruct((bh, chunks, dh), F32)] * 2,
        compiler_params=_params(("arbitrary",)), name="nsa_compress",
    )(*ins)


NSA_TQ = 256
NSA_TK = 256


def _masked_softmax(s, mask):
    s = jnp.where(mask, s, NEG_INF)
    m = jnp.max(s, -1, keepdims=True)
    e = jnp.exp(s - m) * mask.astype(F32)
    return e / jnp.maximum(jnp.sum(e, -1, keepdims=True), 1e-30)


def _attn_kernel(q_ref, gate_ref, kc_ref, vc_ref, ks_ref, vs_ref, kw_ref, vw_ref, o_ref, s_scr, m_scr, l_scr, acc_scr):
    g, tq, dh = q_ref.shape[1:]
    m_rows = g * tq
    n_cmp_pad = kc_ref.shape[1]
    n_slc = ks_ref.shape[1] // SLC_BLOCK
    qi = pl.program_id(1)
    t0 = qi * tq
    qb = q_ref[0].reshape(m_rows, dh).astype(BF16)
    t_col = t0 + lax.broadcasted_iota(I32, (tq, 1), 0)
    t_rows = jnp.concatenate([t_col] * g, axis=0)

    c_idx = lax.broadcasted_iota(I32, (tq, n_cmp_pad), 1)
    mask_c = (c_idx * CMP_STRIDE + CMP_BLOCK - 1 <= t_col) & (c_idx < n_cmp_pad - 1)
    s_c = _nt(qb, kc_ref[0].astype(BF16)).reshape(g, tq, n_cmp_pad)
    p_c = _masked_softmax(s_c, jnp.broadcast_to(mask_c[None], s_c.shape))
    o_cmp = _mm(p_c.reshape(m_rows, n_cmp_pad).astype(BF16), vc_ref[0].astype(BF16))

    p_sum = jnp.sum(p_c, axis=0)
    n_i = lax.broadcasted_iota(I32, (n_slc, n_cmp_pad), 0) * SLC_BLOCK
    c_s = lax.broadcasted_iota(I32, (n_slc, n_cmp_pad), 1) * CMP_STRIDE
    overlap_t = ((c_s < n_i + SLC_BLOCK) & (c_s + CMP_BLOCK > n_i)).astype(F32)
    imp_t = _nt(overlap_t, p_sum, precision=lax.Precision.HIGHEST)
    blk = lax.broadcasted_iota(I32, (n_slc, tq), 0)
    t_lane = t0 + lax.broadcasted_iota(I32, (n_slc, tq), 1)
    d_cur = t_lane // SLC_BLOCK - blk
    forced = (blk == 0) | ((d_cur >= 0) & (d_cur < SLC_FORCED_LOCAL))
    score = jnp.where(forced, FORCE_BONUS, jnp.where(blk * SLC_BLOCK <= t_lane, imp_t, NEG_INF))
    rank = jnp.zeros((n_slc, tq), I32)
    for m in range(n_slc):
        sm = score[m:m + 1, :]
        rank = rank + ((sm > score) | ((sm == score) & (blk > m))).astype(I32)
    pen_t = jnp.where(rank < min(SLC_TOP_N, n_slc), 0.0, SEL_PENALTY).astype(BF16)
    eye = (lax.broadcasted_iota(I32, (tq, tq), 0) == lax.broadcasted_iota(I32, (tq, tq), 1)).astype(BF16)
    pen = _nt(eye, pen_t).astype(BF16)
    pen_rows = jnp.concatenate([pen] * g, axis=0)

    assert tq == NSA_TK and NSA_TK % LANES == 0
    lane_blocks = NSA_TK // LANES
    m_scr[...] = jnp.full(m_scr.shape, NEG_INF, F32)

    def score_tile(kt, diagonal):
        k0 = pl.multiple_of(kt * NSA_TK, NSA_TK)
        kb = ks_ref[0, pl.ds(k0, NSA_TK), :].astype(BF16)
        j_pos = k0 + lax.broadcasted_iota(I32, (n_slc, NSA_TK), 1)
        expand = (j_pos // SLC_BLOCK == lax.broadcasted_iota(I32, (n_slc, NSA_TK), 0)).astype(BF16)
        s = _nt(qb, kb) + _mm(pen_rows, expand)
        if diagonal:
            key_pos = k0 + lax.broadcasted_iota(I32, (m_rows, NSA_TK), 1)
            s = jnp.where(key_pos <= t_rows, s, NEG_INF)
        s_scr[kt] = s
        mx = m_scr[...]
        for c in range(lane_blocks):
            mx = jnp.maximum(mx, s[:, c * LANES:(c + 1) * LANES])
        m_scr[...] = mx

    def score_step(kt, _):
        score_tile(kt, False)
        return 0

    lax.fori_loop(0, qi, score_step, 0)
    score_tile(qi, True)
    m_scr[...] = jnp.broadcast_to(jnp.max(m_scr[...], -1, keepdims=True), m_scr.shape)
    l_scr[...] = jnp.zeros_like(l_scr)
    acc_scr[...] = jnp.zeros_like(acc_scr)

    def value_step(kt, _):
        k0 = pl.multiple_of(kt * NSA_TK, NSA_TK)
        vb = vs_ref[0, pl.ds(k0, NSA_TK), :].astype(BF16)
        s = s_scr[kt]
        row_max = m_scr[...]
        lsum = l_scr[...]
        parts = []
        for c in range(lane_blocks):
            p = jnp.exp(s[:, c * LANES:(c + 1) * LANES] - row_max)
            lsum = lsum + p
            parts.append(p.astype(BF16))
        l_scr[...] = lsum
        acc_scr[...] += _mm(jnp.concatenate(parts, axis=1), vb)
        return 0

    lax.fori_loop(0, qi + 1, value_step, 0)
    o_sel = acc_scr[...] / jnp.sum(l_scr[...], -1, keepdims=True)

    span = WINDOW + tq
    kwin = kw_ref[0, pl.ds(pl.multiple_of(t0, tq), span), :].astype(BF16)
    vwin = vw_ref[0, pl.ds(pl.multiple_of(t0, tq), span), :].astype(BF16)
    kpos = t0 - WINDOW + lax.broadcasted_iota(I32, (m_rows, span), 1)
    diff = t_rows - kpos
    p_w = _masked_softmax(_nt(qb, kwin), (diff >= 0) & (diff < WINDOW) & (kpos >= 0))
    o_win = _mm(p_w.astype(BF16), vwin)

    gate = gate_ref[0].reshape(m_rows, 3)
    o = gate[:, 0:1] * o_cmp + gate[:, 1:2] * o_sel + gate[:, 2:3] * o_win
    o_ref[0] = o.reshape(g, tq, dh)


def _attention(q5, gates, k_cmp, v_cmp, ks, vs, kw_pad, vw_pad):
    bh, g, t, dh = q5.shape
    tq = NSA_TQ
    n_cmp_pad = k_cmp.shape[1]
    m_rows = g * tq
    qblk = pl.BlockSpec((1, g, tq, dh), lambda b, i: (b, 0, i, 0))
    cblk = pl.BlockSpec((1, n_cmp_pad, dh), lambda b, i: (b, 0, 0))
    kblk = pl.BlockSpec((1, t, dh), lambda b, i: (b, 0, 0))
    wblk = pl.BlockSpec((1, t + WINDOW, dh), lambda b, i: (b, 0, 0))
    return pl.pallas_call(
        _attn_kernel, grid=(bh, t // tq),
        in_specs=[qblk, pl.BlockSpec((1, g, tq, 3), lambda b, i: (b, 0, i, 0)), cblk, cblk, kblk, kblk, wblk, wblk],
        out_specs=qblk,
        out_shape=jax.ShapeDtypeStruct((bh, g, t, dh), F32),
        scratch_shapes=[pltpu.VMEM((t // NSA_TK, m_rows, NSA_TK), F32), pltpu.VMEM((m_rows, LANES), F32),
                        pltpu.VMEM((m_rows, LANES), F32), pltpu.VMEM((m_rows, dh), F32)],
        compiler_params=_params(("arbitrary", "arbitrary")), name="nsa_attention",
    )(q5, gates, k_cmp, v_cmp, ks, vs, kw_pad, vw_pad)


def _outproj_kernel(o_ref, x_ref, w_ref, g_ref, b_ref, out_ref, wb):
    @pl.when(pl.program_id(0) == 0)
    def _():
        wb[...] = w_ref[...].astype(BF16)

    mix = _mm(o_ref[...].astype(BF16), wb[...])
    out_ref[...] = _layer_norm(ALPHA * x_ref[...] + mix, g_ref[...], b_ref[...])


def _outproj(o, x, w_out, ln_g, ln_b):
    n, d = x.shape
    tm = 512
    blk = pl.BlockSpec((tm, d), lambda i: (i, 0))
    return pl.pallas_call(
        _outproj_kernel, grid=(n // tm,),
        in_specs=[blk, blk, _full(w_out.shape), _full((1, d)), _full((1, d))],
        out_specs=blk, out_shape=jax.ShapeDtypeStruct((n, d), F32),
        scratch_shapes=[pltpu.VMEM(w_out.shape, BF16)],
        compiler_params=_params(("arbitrary",)), name="nsa_outproj",
    )(o, x, w_out, ln_g[None], ln_b[None])


def _nsa_mixer(x, batch, seq, w_in, pe_k, pe_v, wk1, wk2, wv1, wv2, w_out, ln_g, ln_b):
    n, d = x.shape
    h, hk, g = NSA_HEADS, NSA_KV_HEADS, NSA_GROUP
    dh = d // h
    qd, kd = h * dh, hk * dh
    q, kv, gates = _nsa_proj(x, w_in, qd, 6 * kd)
    q5 = q.reshape(batch, seq, hk, g, dh).transpose(0, 2, 3, 1, 4).reshape(batch * hk, g, seq, dh)
    gates = gates.reshape(batch, seq, hk, g, 3).transpose(0, 2, 3, 1, 4).reshape(batch * hk, g, seq, 3)
    kv = kv.reshape(batch, seq, 6, hk, dh).transpose(2, 0, 3, 1, 4).reshape(6, batch * hk, seq, dh)
    kc, vc, ks, vs, kw, vw = (kv[i] for i in range(6))
    chunks = seq // CMP_STRIDE
    k_cmp, v_cmp = _compress(kc.reshape(batch * hk, chunks, CMP_STRIDE * dh), vc.reshape(batch * hk, chunks, CMP_STRIDE * dh),
                             pe_k, pe_v, wk1, wk2, wv1, wv2)
    pad = ((0, 0), (WINDOW, 0), (0, 0))
    o5 = _attention(q5, gates, k_cmp, v_cmp, ks, vs, jnp.pad(kw, pad), jnp.pad(vw, pad))
    o = o5.reshape(batch, hk, g, seq, dh).transpose(0, 3, 1, 2, 4).reshape(n, d)
    return _outproj(o, x, w_out, ln_g, ln_b)


def kernel(x, ln_mix_g, ln_mix_b, ln_ffn_g, ln_ffn_b, conf_w_in, conf_w_dw, conf_b_dw, conf_ln_g, conf_ln_b, conf_w_out, sc_w_in, sc_w_conv, sc_w_out, nsa_w_in, nsa_pe_k, nsa_pe_v, nsa_wk1, nsa_wk2, nsa_wv1, nsa_wv2, nsa_w_out, ffn_w_gate, ffn_w_up, ffn_w_down, moe_w_router, moe_w_gate, moe_w_up, moe_w_down):
    batch, seq, d = x.shape
    h = x.reshape(batch * seq, d)
    for i in range(DEPTH):
        kind, j = i % N_MIXERS, i // N_MIXERS
        f = i // 2
        w_router = moe_w_router[f] if i % 2 == 1 else None
        if kind == 0:
            res = _mixer("conformer", h, (conf_w_in[j], conf_w_dw[j], conf_b_dw[j], conf_ln_g[j], conf_ln_b[j], conf_w_out[j]),
                         ln_mix_g[i], ln_mix_b[i], seq, w_router)
        elif kind == 1:
            res = _mixer("shortconv", h, (sc_w_in[j], sc_w_conv[j], sc_w_out[j]), ln_mix_g[i], ln_mix_b[i], seq, w_router)
        else:
            assert w_router is None
            res = [_nsa_mixer(h, batch, seq, nsa_w_in[j], nsa_pe_k[j], nsa_pe_v[j], nsa_wk1[j], nsa_wk2[j],
                              nsa_wv1[j], nsa_wv2[j], nsa_w_out[j], ln_mix_g[i], ln_mix_b[i])]
        h = res[0]
        if i % 2 == 0:
            h = _dense_ffn(h, ffn_w_gate[f], ffn_w_up[f], ffn_w_down[f], ln_ffn_g[i], ln_ffn_b[i])
        else:
            _, idx, rank, wts, cnt = res
            h = _moe(h, idx, rank, wts, cnt[:, 0], moe_w_gate[f], moe_w_up[f], moe_w_down[f], ln_ffn_g[i], ln_ffn_b[i])
    return h.reshape(batch, seq, d)
```

```python
import functools

import jax
import jax.numpy as jnp
from jax import lax
from jax.experimental import pallas as pl
from jax.experimental.pallas import tpu as pltpu

F32 = jnp.float32
BF16 = jnp.bfloat16
I32 = jnp.int32

DEPTH = 4
N_MIXERS = 3
ALPHA = (2.0 * DEPTH) ** 0.25
LN_EPS = 1e-5
NEG_INF = -1e30

CONV_KERNEL = 31
SHORT_KERNEL = 3
NSA_HEADS = 16
NSA_KV_HEADS = 4
NSA_GROUP = NSA_HEADS // NSA_KV_HEADS
CMP_BLOCK = 32
CMP_STRIDE = 16
SLC_BLOCK = 64
SLC_TOP_N = 16
SLC_FORCED_LOCAL = 2
FORCE_BONUS = 1e4
WINDOW = 512
N_EXPERTS = 8
TOP_K = 2

LANES = 128
SUBLANES = 8
VMEM_LIMIT_BYTES = 56 * 1024 * 1024

HALO = 32
CONV_ROWS = 16
SEL_PENALTY = -1e9


def _nt(a, b, **kw):
    return lax.dot_general(a, b, (((1,), (1,)), ((), ())), preferred_element_type=F32, **kw)


def _mm(a, b):
    return jnp.dot(a, b, preferred_element_type=F32)


def _layer_norm(v, g, b):
    mu = jnp.mean(v, -1, keepdims=True)
    d = v - mu
    var = jnp.mean(d * d, -1, keepdims=True)
    return d * lax.rsqrt(var + LN_EPS) * g + b


def _full(shape):
    return pl.BlockSpec(shape, lambda *_: (0,) * len(shape), pipeline_mode=pl.Buffered(1))


def _params(semantics):
    return pltpu.CompilerParams(dimension_semantics=semantics, vmem_limit_bytes=VMEM_LIMIT_BYTES)


def _route(y, wr_t, carry_ref, idx_ref, rank_ref, wts_ref, cnt_ref):
    tm = y.shape[0]
    logits = _nt(wr_t, y, precision=lax.Precision.HIGHEST)
    eidx = lax.broadcasted_iota(I32, logits.shape, 0).astype(F32)
    m1 = jnp.max(logits, axis=0, keepdims=True)
    i1 = jnp.min(jnp.where(logits == m1, eidx, float(N_EXPERTS)), axis=0, keepdims=True)
    rest = jnp.where(eidx == i1, -jnp.inf, logits)
    m2 = jnp.max(rest, axis=0, keepdims=True)
    i2 = jnp.min(jnp.where(rest == m2, eidx, float(N_EXPERTS)), axis=0, keepdims=True)
    e2 = jnp.exp(m2 - m1)
    denom = 1.0 + e2
    chosen = (eidx == i1) | (eidx == i2)
    chosen_f = chosen.astype(F32)
    upper = (lax.broadcasted_iota(I32, (tm, tm), 0) < lax.broadcasted_iota(I32, (tm, tm), 1)).astype(BF16)
    before = _mm(chosen_f.astype(BF16), upper) + carry_ref[...]
    r1 = jnp.sum(jnp.where(eidx == i1, before, 0.0), axis=0, keepdims=True)
    r2 = jnp.sum(jnp.where(eidx == i2, before, 0.0), axis=0, keepdims=True)
    carry_ref[...] = carry_ref[...] + jnp.sum(chosen_f, axis=1, keepdims=True)
    idx_ref[...] = jnp.concatenate([i1, i2], axis=0).astype(I32)
    rank_ref[...] = jnp.concatenate([r1, r2], axis=0).astype(I32)
    wts_ref[...] = jnp.concatenate([1.0 / denom, e2 / denom], axis=0)
    cnt_ref[...] = jnp.broadcast_to(carry_ref[...], cnt_ref.shape)


def _router_out(n, tm):
    shapes = [jax.ShapeDtypeStruct((TOP_K, n), I32), jax.ShapeDtypeStruct((TOP_K, n), I32),
              jax.ShapeDtypeStruct((TOP_K, n), F32), jax.ShapeDtypeStruct((N_EXPERTS, LANES), F32)]
    specs = [pl.BlockSpec((TOP_K, tm), lambda i: (0, i))] * 3 + [pl.BlockSpec((N_EXPERTS, LANES), lambda i: (0, 0))]
    return shapes, specs


def _tap_offsets(ksize):
    return [HALO - (ksize - 1) + k for k in range(ksize)]


def _tap_residues(ksize):
    return sorted({off % SUBLANES for off in _tap_offsets(ksize)})


def _store_shifted(ubuf, u, tm, ksize, first_tile):
    d = u.shape[-1]
    for n, r in enumerate(_tap_residues(ksize)):
        keep = HALO - r

        @pl.when(first_tile)
        def _():
            ubuf[n, 0:keep, :] = jnp.zeros((keep, d), F32)

        @pl.when(jnp.logical_not(first_tile))
        def _():
            ubuf[n, 0:keep, :] = ubuf[n, tm:tm + keep, :]

        ubuf[n, keep:keep + tm, :] = u


def _causal_conv(ubuf, wdw_ref, out_ref, bias, tm, ksize):
    d = out_ref.shape[-1]
    residues = _tap_residues(ksize)

    groups = CONV_ROWS // SUBLANES

    def chunk(c, _):
        r0 = pl.multiple_of(c * CONV_ROWS, CONV_ROWS)
        acc = jnp.broadcast_to(bias, (groups, SUBLANES, d))
        for k, off in enumerate(_tap_offsets(ksize)):
            r = off % SUBLANES
            win = ubuf[residues.index(r), pl.ds(r0 + (off - r), CONV_ROWS), :]
            acc = acc + wdw_ref[k][None] * win.reshape(groups, SUBLANES, d)
        out_ref[pl.ds(r0, CONV_ROWS), :] = acc.reshape(CONV_ROWS, d)
        return 0

    lax.fori_loop(0, tm // CONV_ROWS, chunk, 0)


def _mixer_kernel(*refs, kind, tm, tiles_per_seq, with_router):
    if kind == "conformer":
        (x_ref, win_ref, wdw_ref, bdw_ref, cg_ref, cb_ref, wout_ref, g_ref, b_ref), refs = refs[:9], refs[9:]
    else:
        (x_ref, win_ref, wdw_ref, wout_ref, g_ref, b_ref), refs = refs[:6], refs[6:]
    if with_router:
        wr_ref, refs = refs[0], refs[1:]
        out_ref, idx_ref, rank_ref, wts_ref, cnt_ref = refs[:5]
        winb, woutb, ubuf, cbuf, wtap, carry = refs[5:]
    else:
        out_ref = refs[0]
        winb, woutb, ubuf, cbuf, wtap = refs[1:]
    i = pl.program_id(0)
    d = x_ref.shape[-1]

    @pl.when(i == 0)
    def _():
        winb[...] = win_ref[...].astype(BF16)
        woutb[...] = wout_ref[...].astype(BF16)
        for k in range(wdw_ref.shape[0]):
            wtap[k] = jnp.broadcast_to(wdw_ref[k:k + 1, :], (SUBLANES, d))
        if with_router:
            carry[...] = jnp.zeros_like(carry)

    first_tile = i % tiles_per_seq == 0
    x = x_ref[...]
    h = _mm(x.astype(BF16), winb[...])
    if kind == "conformer":
        _store_shifted(ubuf, h[:, :d] * jax.nn.sigmoid(h[:, d:]), tm, CONV_KERNEL, first_tile)
        _causal_conv(ubuf, wtap, cbuf, bdw_ref[...], tm, CONV_KERNEL)
        v = jax.nn.silu(_layer_norm(cbuf[...], cg_ref[...], cb_ref[...]))
    else:
        _store_shifted(ubuf, h[:, d:2 * d] * h[:, 2 * d:], tm, SHORT_KERNEL, first_tile)
        _causal_conv(ubuf, wtap, cbuf, jnp.zeros((1, d), F32), tm, SHORT_KERNEL)
        v = h[:, :d] * cbuf[...]
    mix = _mm(v.astype(BF16), woutb[...])
    y = _layer_norm(ALPHA * x + mix, g_ref[...], b_ref[...])
    out_ref[...] = y
    if with_router:
        _route(y, wr_ref[...], carry, idx_ref, rank_ref, wts_ref, cnt_ref)


def _mixer(kind, x, weights, ln_g, ln_b, seq_len, w_router=None):
    n, d = x.shape
    tm = 256 if kind == "conformer" else 512
    ksize = CONV_KERNEL if kind == "conformer" else SHORT_KERNEL
    assert seq_len % tm == 0 and n % seq_len == 0
    with_router = w_router is not None
    if kind == "conformer":
        w_in, w_dw, b_dw, cg, cb, w_out = weights
        ins = [x, w_in, w_dw, b_dw[None], cg[None], cb[None], w_out, ln_g[None], ln_b[None]]
    else:
        w_in, w_dw, w_out = weights
        ins = [x, w_in, w_dw, w_out, ln_g[None], ln_b[None]]
    in_specs = [pl.BlockSpec((tm, d), lambda i: (i, 0))] + [_full(a.shape) for a in ins[1:]]
    out_shape = [jax.ShapeDtypeStruct((n, d), F32)]
    out_specs = [pl.BlockSpec((tm, d), lambda i: (i, 0))]
    scratch = [pltpu.VMEM(w_in.shape, BF16), pltpu.VMEM(w_out.shape, BF16),
               pltpu.VMEM((len(_tap_residues(ksize)), tm + HALO, d), F32), pltpu.VMEM((tm, d), F32),
               pltpu.VMEM((ksize, SUBLANES, d), F32)]
    if with_router:
        wr_t = w_router.T
        ins.append(wr_t)
        in_specs.append(_full(wr_t.shape))
        rs, rspec = _router_out(n, tm)
        out_shape += rs
        out_specs += rspec
        scratch.append(pltpu.VMEM((N_EXPERTS, 1), F32))
    return pl.pallas_call(
        functools.partial(_mixer_kernel, kind=kind, tm=tm, tiles_per_seq=seq_len // tm, with_router=with_router),
        grid=(n // tm,), in_specs=in_specs, out_specs=out_specs, out_shape=out_shape,
        scratch_shapes=scratch, compiler_params=_params(("arbitrary",)), name=f"mixer_{kind}",
    )(*ins)


def _ffn_kernel(x_ref, wg_ref, wu_ref, wd_ref, g_ref, b_ref, out_ref, xb, acc):
    j = pl.program_id(1)

    @pl.when(j == 0)
    def _():
        xb[...] = x_ref[...].astype(BF16)
        acc[...] = jnp.zeros_like(acc)

    hg = _mm(xb[...], wg_ref[...].astype(BF16))
    hu = _mm(xb[...], wu_ref[...].astype(BF16))
    acc[...] += _mm((jax.nn.silu(hg) * hu).astype(BF16), wd_ref[...].astype(BF16))

    @pl.when(j == pl.num_programs(1) - 1)
    def _():
        out_ref[...] = _layer_norm(ALPHA * x_ref[...] + acc[...], g_ref[...], b_ref[...])


def _dense_ffn(x, w_gate, w_up, w_down, ln_g, ln_b):
    n, d = x.shape
    dff = w_gate.shape[1]
    tm, tf = 1024, 256
    assert n % tm == 0 and dff % tf == 0
    return pl.pallas_call(
        _ffn_kernel, grid=(n // tm, dff // tf),
        in_specs=[pl.BlockSpec((tm, d), lambda i, j: (i, 0)),
                  pl.BlockSpec((d, tf), lambda i, j: (0, j)),
                  pl.BlockSpec((d, tf), lambda i, j: (0, j)),
                  pl.BlockSpec((tf, d), lambda i, j: (j, 0)),
                  pl.BlockSpec((1, d), lambda i, j: (0, 0)),
                  pl.BlockSpec((1, d), lambda i, j: (0, 0))],
        out_specs=pl.BlockSpec((tm, d), lambda i, j: (i, 0)),
        out_shape=jax.ShapeDtypeStruct((n, d), F32),
        scratch_shapes=[pltpu.VMEM((tm, d), BF16), pltpu.VMEM((tm, d), F32)],
        compiler_params=_params(("arbitrary", "arbitrary")), name="dense_ffn",
    )(x, w_gate, w_up, w_down, ln_g[None], ln_b[None])


MOE_TM = 1024
MOE_TF = 512
MOE_TB = 256


def _row_copy(src, dst, s, t, sem):
    return pltpu.make_async_copy(src.at[pl.ds(s, 1), :], dst.at[pl.ds(t, 1), :], sem)


def _dispatch_kernel(d0_ref, d1_ref, x_ref, zeros_hbm, xs_hbm, sem):
    del zeros_hbm
    base = pl.program_id(0) * MOE_TB

    def start(t, _):
        _row_copy(x_ref, xs_hbm, t, d0_ref[base + t], sem).start()
        _row_copy(x_ref, xs_hbm, t, d1_ref[base + t], sem).start()
        return 0

    def wait(t, _):
        _row_copy(x_ref, xs_hbm, t, d0_ref[base + t], sem).wait()
        _row_copy(x_ref, xs_hbm, t, d1_ref[base + t], sem).wait()
        return 0

    lax.fori_loop(0, MOE_TB, start, 0, unroll=8)
    lax.fori_loop(0, MOE_TB, wait, 0, unroll=8)


def _dispatch(x, dest0, dest1, rows):
    n, d = x.shape
    return pl.pallas_call(
        _dispatch_kernel,
        grid_spec=pltpu.PrefetchScalarGridSpec(
            num_scalar_prefetch=2, grid=(n // MOE_TB,),
            in_specs=[pl.BlockSpec((MOE_TB, d), lambda i, d0, d1: (i, 0)), pl.BlockSpec(memory_space=pl.ANY)],
            out_specs=pl.BlockSpec(memory_space=pl.ANY),
            scratch_shapes=[pltpu.SemaphoreType.DMA(())]),
        out_shape=jax.ShapeDtypeStruct((rows, d), F32),
        input_output_aliases={3: 0},
        compiler_params=_params(("arbitrary",)), name="moe_dispatch",
    )(dest0, dest1, x, jnp.zeros((rows, d), F32))


def _expert_kernel(te_ref, tx_ref, tv_ref, x_ref, wg_ref, wu_ref, wd_ref, out_ref, xb, acc):
    del te_ref, tx_ref
    i, j = pl.program_id(0), pl.program_id(1)

    @pl.when(tv_ref[i] == 1)
    def _():
        @pl.when(j == 0)
        def _():
            xb[...] = x_ref[...].astype(BF16)
            acc[...] = jnp.zeros_like(acc)

        hg = _mm(xb[...], wg_ref[0].astype(BF16))
        hu = _mm(xb[...], wu_ref[0].astype(BF16))
        acc[...] += _mm((jax.nn.silu(hg) * hu).astype(BF16), wd_ref[0].astype(BF16))

        @pl.when(j == pl.num_programs(1) - 1)
        def _():
            out_ref[...] = acc[...]

    @pl.when((tv_ref[i] == 0) & (j == pl.num_programs(1) - 1))
    def _():
        out_ref[...] = jnp.zeros_like(out_ref)


def _experts(xs, tile_expert, tile_row, tile_valid, w_gate, w_up, w_down):
    rows, d = xs.shape
    dff = w_gate.shape[2]
    n_ff = dff // MOE_TF
    assert dff % MOE_TF == 0 and rows % MOE_TM == 0

    def jj(i, j, tv):
        return jnp.where(tv[i] == 1, j, n_ff - 1)

    return pl.pallas_call(
        _expert_kernel,
        grid_spec=pltpu.PrefetchScalarGridSpec(
            num_scalar_prefetch=3, grid=(rows // MOE_TM, n_ff),
            in_specs=[pl.BlockSpec((MOE_TM, d), lambda i, j, te, tx, tv: (tx[i], 0)),
                      pl.BlockSpec((1, d, MOE_TF), lambda i, j, te, tx, tv: (te[i], 0, jj(i, j, tv))),
                      pl.BlockSpec((1, d, MOE_TF), lambda i, j, te, tx, tv: (te[i], 0, jj(i, j, tv))),
                      pl.BlockSpec((1, MOE_TF, d), lambda i, j, te, tx, tv: (te[i], jj(i, j, tv), 0))],
            out_specs=pl.BlockSpec((MOE_TM, d), lambda i, j, te, tx, tv: (i, 0)),
            scratch_shapes=[pltpu.VMEM((MOE_TM, d), BF16), pltpu.VMEM((MOE_TM, d), F32)]),
        out_shape=jax.ShapeDtypeStruct((rows, d), F32),
        compiler_params=_params(("arbitrary", "arbitrary")), name="moe_experts",
    )(tile_expert, tile_row, tile_valid, xs, w_gate, w_up, w_down)


def _combine_kernel(d0_ref, d1_ref, x_ref, w_ref, g_ref, b_ref, ys_hbm, out_ref, ybuf, sem):
    base = pl.program_id(0) * MOE_TB

    def start(t, _):
        _row_copy(ys_hbm, ybuf.at[0], d0_ref[base + t], t, sem).start()
        _row_copy(ys_hbm, ybuf.at[1], d1_ref[base + t], t, sem).start()
        return 0

    def wait(t, _):
        _row_copy(ys_hbm, ybuf.at[0], d0_ref[base + t], t, sem).wait()
        _row_copy(ys_hbm, ybuf.at[1], d1_ref[base + t], t, sem).wait()
        return 0

    lax.fori_loop(0, MOE_TB, start, 0, unroll=8)
    lax.fori_loop(0, MOE_TB, wait, 0, unroll=8)
    w = w_ref[...]
    ffn = w[:, 0:1] * ybuf[0] + w[:, 1:2] * ybuf[1]
    out_ref[...] = _layer_norm(ALPHA * x_ref[...] + ffn, g_ref[...], b_ref[...])


def _combine(x, ys, dest0, dest1, wts, ln_g, ln_b):
    n, d = x.shape
    return pl.pallas_call(
        _combine_kernel,
        grid_spec=pltpu.PrefetchScalarGridSpec(
            num_scalar_prefetch=2, grid=(n // MOE_TB,),
            in_specs=[pl.BlockSpec((MOE_TB, d), lambda i, d0, d1: (i, 0)),
                      pl.BlockSpec((MOE_TB, TOP_K), lambda i, d0, d1: (i, 0)),
                      pl.BlockSpec((1, d), lambda i, d0, d1: (0, 0)),
                      pl.BlockSpec((1, d), lambda i, d0, d1: (0, 0)),
                      pl.BlockSpec(memory_space=pl.ANY)],
            out_specs=pl.BlockSpec((MOE_TB, d), lambda i, d0, d1: (i, 0)),
            scratch_shapes=[pltpu.VMEM((TOP_K, MOE_TB, d), F32), pltpu.SemaphoreType.DMA(())]),
        out_shape=jax.ShapeDtypeStruct((n, d), F32),
        compiler_params=_params(("arbitrary",)), name="moe_combine",
    )(dest0, dest1, x, wts, ln_g[None], ln_b[None], ys)


def _moe(x, idx, rank, wts, counts, w_gate, w_up, w_down, ln_g, ln_b):
    n, _ = x.shape
    n_tiles = (TOP_K * n) // MOE_TM + N_EXPERTS
    rows = n_tiles * MOE_TM
    cnt = counts.astype(I32)
    padded = ((cnt + MOE_TM - 1) // MOE_TM) * MOE_TM
    ends = jnp.cumsum(padded)
    offsets = ends - padded
    expert_offset = jnp.sum(jnp.where(idx[..., None] == jnp.arange(N_EXPERTS, dtype=I32), offsets, 0), axis=-1)
    dest = expert_offset + rank
    tile_start = jnp.arange(n_tiles, dtype=I32) * MOE_TM
    tile_valid = (tile_start < ends[-1]).astype(I32)
    last = jnp.maximum(ends[-1] // MOE_TM - 1, 0)
    tile_row = jnp.minimum(jnp.arange(n_tiles, dtype=I32), last)
    tile_expert = jnp.minimum(jnp.sum(tile_row[:, None] * MOE_TM >= ends[None, :], axis=1), N_EXPERTS - 1).astype(I32)
    xs = _dispatch(x, dest[0], dest[1], rows)
    ys = _experts(xs, tile_expert, tile_row, tile_valid, w_gate, w_up, w_down)
    return _combine(x, ys, dest[0], dest[1], wts.T, ln_g, ln_b)


def _nsa_proj_kernel(x_ref, win_ref, q_ref, kv_ref, gate_ref, winb, *, qd, kvd, scale):
    @pl.when(pl.program_id(0) == 0)
    def _():
        winb[...] = win_ref[...].astype(BF16)

    h = _mm(x_ref[...].astype(BF16), winb[...])
    q_ref[...] = h[:, :qd] * scale
    kv_ref[...] = h[:, qd:qd + kvd]
    gate_ref[...] = jax.nn.sigmoid(h[:, qd + kvd:])


def _nsa_proj(x, w_in, qd, kvd):
    n, d = x.shape
    cols = w_in.shape[1]
    gd = cols - qd - kvd
    tm = 512
    dh = qd // NSA_HEADS
    return pl.pallas_call(
        functools.partial(_nsa_proj_kernel, qd=qd, kvd=kvd, scale=dh ** -0.5),
        grid=(n // tm,),
        in_specs=[pl.BlockSpec((tm, d), lambda i: (i, 0)), _full(w_in.shape)],
        out_specs=[pl.BlockSpec((tm, qd), lambda i: (i, 0)), pl.BlockSpec((tm, kvd), lambda i: (i, 0)),
                   pl.BlockSpec((tm, gd), lambda i: (i, 0))],
        out_shape=[jax.ShapeDtypeStruct((n, qd), F32), jax.ShapeDtypeStruct((n, kvd), F32),
                   jax.ShapeDtypeStruct((n, gd), F32)],
        scratch_shapes=[pltpu.VMEM(w_in.shape, BF16)],
        compiler_params=_params(("arbitrary",)), name="nsa_proj",
    )(x, w_in)


def _compress_kernel(ck_ref, cv_ref, pek_ref, pev_ref, wk1_ref, wk1c_ref, wk2_ref, wv1_ref, wv1c_ref, wv2_ref,
                     ko_ref, vo_ref):
    def one(c_ref, pe_ref, w1_ref, w1c_ref, w2_ref, o_ref):
        hid = w2_ref.shape[0]
        a = _mm(c_ref[0].astype(BF16), w1c_ref[...].astype(BF16))
        bias = _mm(pe_ref[...].astype(BF16), w1_ref[...].astype(BF16))[0:1, :]
        pre = a[:, :hid] + pltpu.roll(a[:, hid:], a.shape[0] - 1, 0) + bias
        o_ref[0] = _mm(jax.nn.gelu(pre).astype(BF16), w2_ref[...].astype(BF16))

    one(ck_ref, pek_ref, wk1_ref, wk1c_ref, wk2_ref, ko_ref)
    one(cv_ref, pev_ref, wv1_ref, wv1c_ref, wv2_ref, vo_ref)


def _compress(ck, cv, pe_k, pe_v, wk1, wk2, wv1, wv2):
    bh, chunks, cw = ck.shape
    dh = wk2.shape[1]

    def prep(pe, w1):
        pe_rows = jnp.zeros((SUBLANES, pe.size), F32).at[0].set(pe.reshape(-1))
        return pe_rows, jnp.concatenate([w1[:cw], w1[cw:]], axis=1)

    pek, wk1c = prep(pe_k, wk1)
    pev, wv1c = prep(pe_v, wv1)
    ins = [ck, cv, pek, pev, wk1, wk1c, wk2, wv1, wv1c, wv2]
    blk = pl.BlockSpec((1, chunks, cw), lambda i: (i, 0, 0))
    oblk = pl.BlockSpec((1, chunks, dh), lambda i: (i, 0, 0))
    return pl.pallas_call(
        _compress_kernel, grid=(bh,),
        in_specs=[blk, blk] + [_full(a.shape) for a in ins[2:]],
        out_specs=[oblk, oblk],
        out_shape=[jax.ShapeDtypeStruct((bh, chunks, dh), F32)] * 2,
        compiler_params=_params(("arbitrary",)), name="nsa_compress",
    )(*ins)


NSA_TQ = 256
NSA_TK = 256


def _masked_softmax(s, mask):
    s = jnp.where(mask, s, NEG_INF)
    m = jnp.max(s, -1, keepdims=True)
    e = jnp.exp(s - m) * mask.astype(F32)
    return e / jnp.maximum(jnp.sum(e, -1, keepdims=True), 1e-30)


def _attn_kernel(q_ref, gate_ref, kc_ref, vc_ref, ks_ref, vs_ref, kw_ref, vw_ref, o_ref, s_scr, m_scr, l_scr, acc_scr):
    g, tq, dh = q_ref.shape[1:]
    m_rows = g * tq
    n_cmp_pad = kc_ref.shape[1]
    n_slc = ks_ref.shape[1] // SLC_BLOCK
    qi = pl.program_id(1)
    t0 = qi * tq
    qb = q_ref[0].reshape(m_rows, dh).astype(BF16)
    t_col = t0 + lax.broadcasted_iota(I32, (tq, 1), 0)
    t_rows = jnp.concatenate([t_col] * g, axis=0)

    c_idx = lax.broadcasted_iota(I32, (tq, n_cmp_pad), 1)
    mask_c = (c_idx * CMP_STRIDE + CMP_BLOCK - 1 <= t_col) & (c_idx < n_cmp_pad - 1)
    s_c = _nt(qb, kc_ref[0].astype(BF16)).reshape(g, tq, n_cmp_pad)
    p_c = _masked_softmax(s_c, jnp.broadcast_to(mask_c[None], s_c.shape))
    o_cmp = _mm(p_c.reshape(m_rows, n_cmp_pad).astype(BF16), vc_ref[0].astype(BF16))

    p_sum = jnp.sum(p_c, axis=0)
    n_i = lax.broadcasted_iota(I32, (n_slc, n_cmp_pad), 0) * SLC_BLOCK
    c_s = lax.broadcasted_iota(I32, (n_slc, n_cmp_pad), 1) * CMP_STRIDE
    overlap_t = ((c_s < n_i + SLC_BLOCK) & (c_s + CMP_BLOCK > n_i)).astype(F32)
    imp_t = _nt(overlap_t, p_sum, precision=lax.Precision.HIGHEST)
    blk = lax.broadcasted_iota(I32, (n_slc, tq), 0)
    t_lane = t0 + lax.broadcasted_iota(I32, (n_slc, tq), 1)
    d_cur = t_lane // SLC_BLOCK - blk
    forced = (blk == 0) | ((d_cur >= 0) & (d_cur < SLC_FORCED_LOCAL))
    score = jnp.where(forced, FORCE_BONUS, jnp.where(blk * SLC_BLOCK <= t_lane, imp_t, NEG_INF))
    rank = jnp.zeros((n_slc, tq), I32)
    for m in range(n_slc):
        sm = score[m:m + 1, :]
        rank = rank + ((sm > score) | ((sm == score) & (blk > m))).astype(I32)
    pen_t = jnp.where(rank < min(SLC_TOP_N, n_slc), 0.0, SEL_PENALTY).astype(BF16)
    eye = (lax.broadcasted_iota(I32, (tq, tq), 0) == lax.broadcasted_iota(I32, (tq, tq), 1)).astype(BF16)
    pen = _nt(eye, pen_t).astype(BF16)
    pen_rows = jnp.concatenate([pen] * g, axis=0)

    assert tq == NSA_TK and NSA_TK % LANES == 0
    lane_blocks = NSA_TK // LANES
    m_scr[...] = jnp.full(m_scr.shape, NEG_INF, F32)

    def score_tile(kt, diagonal):
        k0 = pl.multiple_of(kt * NSA_TK, NSA_TK)
        kb = ks_ref[0, pl.ds(k0, NSA_TK), :].astype(BF16)
        j_pos = k0 + lax.broadcasted_iota(I32, (n_slc, NSA_TK), 1)
        expand = (j_pos // SLC_BLOCK == lax.broadcasted_iota(I32, (n_slc, NSA_TK), 0)).astype(BF16)
        s = _nt(qb, kb) + _mm(pen_rows, expand)
        if diagonal:
            key_pos = k0 + lax.broadcasted_iota(I32, (m_rows, NSA_TK), 1)
            s = jnp.where(key_pos <= t_rows, s, NEG_INF)
        s_scr[kt] = s
        mx = m_scr[...]
        for c in range(lane_blocks):
            mx = jnp.maximum(mx, s[:, c * LANES:(c + 1) * LANES])
        m_scr[...] = mx

    def score_step(kt, _):
        score_tile(kt, False)
        return 0

    lax.fori_loop(0, qi, score_step, 0)
    score_tile(qi, True)
    m_scr[...] = jnp.broadcast_to(jnp.max(m_scr[...], -1, keepdims=True), m_scr.shape)
    l_scr[...] = jnp.zeros_like(l_scr)
    acc_scr[...] = jnp.zeros_like(acc_scr)

    def value_step(kt, _):
        k0 = pl.multiple_of(kt * NSA_TK, NSA_TK)
        vb = vs_ref[0, pl.ds(k0, NSA_TK), :].astype(BF16)
        s = s_scr[kt]
        row_max = m_scr[...]
        lsum = l_scr[...]
        parts = []
        for c in range(lane_blocks):
            p = jnp.exp(s[:, c * LANES:(c + 1) * LANES] - row_max)
            lsum = lsum + p
            parts.append(p.astype(BF16))
        l_scr[...] = lsum
        acc_scr[...] += _mm(jnp.concatenate(parts, axis=1), vb)
        return 0

    lax.fori_loop(0, qi + 1, value_step, 0)
    o_sel = acc_scr[...] / jnp.sum(l_scr[...], -1, keepdims=True)

    span = WINDOW + tq
    kwin = kw_ref[0, pl.ds(pl.multiple_of(t0, tq), span), :].astype(BF16)
    vwin = vw_ref[0, pl.ds(pl.multiple_of(t0, tq), span), :].astype(BF16)
    kpos = t0 - WINDOW + lax.broadcasted_iota(I32, (m_rows, span), 1)
    diff = t_rows - kpos
    p_w = _masked_softmax(_nt(qb, kwin), (diff >= 0) & (diff < WINDOW) & (kpos >= 0))
    o_win = _mm(p_w.astype(BF16), vwin)

    gate = gate_ref[0].reshape(m_rows, 3)
    o = gate[:, 0:1] * o_cmp + gate[:, 1:2] * o_sel + gate[:, 2:3] * o_win
    o_ref[0] = o.reshape(g, tq, dh)


def _attention(q5, gates, k_cmp, v_cmp, ks, vs, kw_pad, vw_pad):
    bh, g, t, dh = q5.shape
    tq = NSA_TQ
    n_cmp_pad = k_cmp.shape[1]
    m_rows = g * tq
    qblk = pl.BlockSpec((1, g, tq, dh), lambda b, i: (b, 0, i, 0))
    cblk = pl.BlockSpec((1, n_cmp_pad, dh), lambda b, i: (b, 0, 0))
    kblk = pl.BlockSpec((1, t, dh), lambda b, i: (b, 0, 0))
    wblk = pl.BlockSpec((1, t + WINDOW, dh), lambda b, i: (b, 0, 0))
    return pl.pallas_call(
        _attn_kernel, grid=(bh, t // tq),
        in_specs=[qblk, pl.BlockSpec((1, g, tq, 3), lambda b, i: (b, 0, i, 0)), cblk, cblk, kblk, kblk, wblk, wblk],
        out_specs=qblk,
        out_shape=jax.ShapeDtypeStruct((bh, g, t, dh), F32),
        scratch_shapes=[pltpu.VMEM((t // NSA_TK, m_rows, NSA_TK), F32), pltpu.VMEM((m_rows, LANES), F32),
                        pltpu.VMEM((m_rows, LANES), F32), pltpu.VMEM((m_rows, dh), F32)],
        compiler_params=_params(("arbitrary", "arbitrary")), name="nsa_attention",
    )(q5, gates, k_cmp, v_cmp, ks, vs, kw_pad, vw_pad)


def _outproj_kernel(o_ref, x_ref, w_ref, g_ref, b_ref, out_ref, wb):
    @pl.when(pl.program_id(0) == 0)
    def _():
        wb[...] = w_ref[...].astype(BF16)

    mix = _mm(o_ref[...].astype(BF16), wb[...])
    out_ref[...] = _layer_norm(ALPHA * x_ref[...] + mix, g_ref[...], b_ref[...])


def _outproj(o, x, w_out, ln_g, ln_b):
    n, d = x.shape
    tm = 512
    blk = pl.BlockSpec((tm, d), lambda i: (i, 0))
    return pl.pallas_call(
        _outproj_kernel, grid=(n // tm,),
        in_specs=[blk, blk, _full(w_out.shape), _full((1, d)), _full((1, d))],
        out_specs=blk, out_shape=jax.ShapeDtypeStruct((n, d), F32),
        scratch_shapes=[pltpu.VMEM(w_out.shape, BF16)],
        compiler_params=_params(("arbitrary",)), name="nsa_outproj",
    )(o, x, w_out, ln_g[None], ln_b[None])


def _nsa_mixer(x, batch, seq, w_in, pe_k, pe_v, wk1, wk2, wv1, wv2, w_out, ln_g, ln_b):
    n, d = x.shape
    h, hk, g = NSA_HEADS, NSA_KV_HEADS, NSA_GROUP
    dh = d // h
    qd, kd = h * dh, hk * dh
    q, kv, gates = _nsa_proj(x, w_in, qd, 6 * kd)
    q5 = q.reshape(batch, seq, hk, g, dh).transpose(0, 2, 3, 1, 4).reshape(batch * hk, g, seq, dh)
    gates = gates.reshape(batch, seq, hk, g, 3).transpose(0, 2, 3, 1, 4).reshape(batch * hk, g, seq, 3)
    kv = kv.reshape(batch, seq, 6, hk, dh).transpose(2, 0, 3, 1, 4).reshape(6, batch * hk, seq, dh)
    kc, vc, ks, vs, kw, vw = (kv[i] for i in range(6))
    chunks = seq // CMP_STRIDE
    k_cmp, v_cmp = _compress(kc.reshape(batch * hk, chunks, CMP_STRIDE * dh), vc.reshape(batch * hk, chunks, CMP_STRIDE * dh),
                             pe_k, pe_v, wk1, wk2, wv1, wv2)
    pad = ((0, 0), (WINDOW, 0), (0, 0))
    o5 = _attention(q5, gates, k_cmp, v_cmp, ks, vs, jnp.pad(kw, pad), jnp.pad(vw, pad))
    o = o5.reshape(batch, hk, g, seq, dh).transpose(0, 3, 1, 2, 4).reshape(n, d)
    return _outproj(o, x, w_out, ln_g, ln_b)


def kernel(x, ln_mix_g, ln_mix_b, ln_ffn_g, ln_ffn_b, conf_w_in, conf_w_dw, conf_b_dw, conf_ln_g, conf_ln_b, conf_w_out, sc_w_in, sc_w_conv, sc_w_out, nsa_w_in, nsa_pe_k, nsa_pe_v, nsa_wk1, nsa_wk2, nsa_wv1, nsa_wv2, nsa_w_out, ffn_w_gate, ffn_w_up, ffn_w_down, moe_w_router, moe_w_gate, moe_w_up, moe_w_down):
    batch, seq, d = x.shape
    h = x.reshape(batch * seq, d)
    for i in range(DEPTH):
        kind, j = i % N_MIXERS, i // N_MIXERS
        f = i // 2
        w_router = moe_w_router[f] if i % 2 == 1 else None
        if kind == 0:
            res = _mixer("conformer", h, (conf_w_in[j], conf_w_dw[j], conf_b_dw[j], conf_ln_g[j], conf_ln_b[j], conf_w_out[j]),
                         ln_mix_g[i], ln_mix_b[i], seq, w_router)
        elif kind == 1:
            res = _mixer("shortconv", h, (sc_w_in[j], sc_w_conv[j], sc_w_out[j]), ln_mix_g[i], ln_mix_b[i], seq, w_router)
        else:
            assert w_router is None
            res = [_nsa_mixer(h, batch, seq, nsa_w_in[j], nsa_pe_k[j], nsa_pe_v[j], nsa_wk1[j], nsa_wk2[j],
                              nsa_wv1[j], nsa_wv2[j], nsa_w_out[j], ln_mix_g[i], ln_mix_b[i])]
        h = res[0]
        if i % 2 == 0:
            h = _dense_ffn(h, ffn_w_gate[f], ffn_w_up[f], ffn_w_down[f], ln_ffn_g[i], ln_ffn_b[i])
        else:
            _, idx, rank, wts, cnt = res
            h = _moe(h, idx, rank, wts, cnt[:, 0], moe_w_gate[f], moe_w_up[f], moe_w_down[f], ln_ffn_g[i], ln_ffn_b[i])
    return h.reshape(batch, seq, d)
```

```python
import functools

import jax
import jax.numpy as jnp
from jax import lax
from jax.experimental import pallas as pl
from jax.experimental.pallas import tpu as pltpu

F32 = jnp.float32
BF16 = jnp.bfloat16
I32 = jnp.int32

DEPTH = 4
N_MIXERS = 3
ALPHA = (2.0 * DEPTH) ** 0.25
LN_EPS = 1e-5
NEG_INF = -1e30

CONV_KERNEL = 31
SHORT_KERNEL = 3
NSA_HEADS = 16
NSA_KV_HEADS = 4
NSA_GROUP = NSA_HEADS // NSA_KV_HEADS
CMP_BLOCK = 32
CMP_STRIDE = 16
SLC_BLOCK = 64
SLC_TOP_N = 16
SLC_FORCED_LOCAL = 2
FORCE_BONUS = 1e4
WINDOW = 512
N_EXPERTS = 8
TOP_K = 2

LANES = 128
SUBLANES = 8
VMEM_LIMIT_BYTES = 56 * 1024 * 1024

HALO = 32
CONV_ROWS = 16
SEL_PENALTY = NEG_INF


def _nt(a, b, **kw):
    return lax.dot_general(a, b, (((1,), (1,)), ((), ())), preferred_element_type=F32, **kw)


def _mm(a, b):
    return jnp.dot(a, b, preferred_element_type=F32)


def _layer_norm(v, g, b):
    mu = jnp.mean(v, -1, keepdims=True)
    d = v - mu
    var = jnp.mean(d * d, -1, keepdims=True)
    return d * lax.rsqrt(var + LN_EPS) * g + b


def _full(shape):
    return pl.BlockSpec(shape, lambda *_: (0,) * len(shape), pipeline_mode=pl.Buffered(1))


def _params(semantics):
    return pltpu.CompilerParams(dimension_semantics=semantics, vmem_limit_bytes=VMEM_LIMIT_BYTES)


def _route(y, wr_t, carry_ref, idx_ref, rank_ref, wts_ref, cnt_ref):
    tm = y.shape[0]
    logits = _nt(wr_t, y, precision=lax.Precision.HIGHEST)
    eidx = lax.broadcasted_iota(I32, logits.shape, 0).astype(F32)
    m1 = jnp.max(logits, axis=0, keepdims=True)
    i1 = jnp.min(jnp.where(logits == m1, eidx, float(N_EXPERTS)), axis=0, keepdims=True)
    rest = jnp.where(eidx == i1, -jnp.inf, logits)
    m2 = jnp.max(rest, axis=0, keepdims=True)
    i2 = jnp.min(jnp.where(rest == m2, eidx, float(N_EXPERTS)), axis=0, keepdims=True)
    e2 = jnp.exp(m2 - m1)
    denom = 1.0 + e2
    chosen = (eidx == i1) | (eidx == i2)
    chosen_f = chosen.astype(F32)
    upper = (lax.broadcasted_iota(I32, (tm, tm), 0) < lax.broadcasted_iota(I32, (tm, tm), 1)).astype(BF16)
    before = _mm(chosen_f.astype(BF16), upper) + carry_ref[...]
    r1 = jnp.sum(jnp.where(eidx == i1, before, 0.0), axis=0, keepdims=True)
    r2 = jnp.sum(jnp.where(eidx == i2, before, 0.0), axis=0, keepdims=True)
    carry_ref[...] = carry_ref[...] + jnp.sum(chosen_f, axis=1, keepdims=True)
    idx_ref[...] = jnp.concatenate([i1, i2], axis=0).astype(I32)
    rank_ref[...] = jnp.concatenate([r1, r2], axis=0).astype(I32)
    wts_ref[...] = jnp.concatenate([1.0 / denom, e2 / denom], axis=0)
    cnt_ref[...] = jnp.broadcast_to(carry_ref[...], cnt_ref.shape)


def _router_out(n, tm):
    shapes = [jax.ShapeDtypeStruct((TOP_K, n), I32), jax.ShapeDtypeStruct((TOP_K, n), I32),
              jax.ShapeDtypeStruct((TOP_K, n), F32), jax.ShapeDtypeStruct((N_EXPERTS, LANES), F32)]
    specs = [pl.BlockSpec((TOP_K, tm), lambda i: (0, i))] * 3 + [pl.BlockSpec((N_EXPERTS, LANES), lambda i: (0, 0))]
    return shapes, specs


def _tap_offsets(ksize):
    return [HALO - (ksize - 1) + k for k in range(ksize)]


def _tap_residues(ksize):
    return sorted({off % SUBLANES for off in _tap_offsets(ksize)})


def _store_shifted(ubuf, u, tm, ksize, first_tile):
    d = u.shape[-1]
    for n, r in enumerate(_tap_residues(ksize)):
        keep = HALO - r

        @pl.when(first_tile)
        def _():
            ubuf[n, 0:keep, :] = jnp.zeros((keep, d), F32)

        @pl.when(jnp.logical_not(first_tile))
        def _():
            ubuf[n, 0:keep, :] = ubuf[n, tm:tm + keep, :]

        ubuf[n, keep:keep + tm, :] = u


def _causal_conv(ubuf, wdw_ref, out_ref, bias, tm, ksize):
    d = out_ref.shape[-1]
    residues = _tap_residues(ksize)

    groups = CONV_ROWS // SUBLANES

    def chunk(c, _):
        r0 = pl.multiple_of(c * CONV_ROWS, CONV_ROWS)
        acc = jnp.broadcast_to(bias, (groups, SUBLANES, d))
        for k, off in enumerate(_tap_offsets(ksize)):
            r = off % SUBLANES
            win = ubuf[residues.index(r), pl.ds(r0 + (off - r), CONV_ROWS), :]
            acc = acc + wdw_ref[k][None] * win.reshape(groups, SUBLANES, d)
        out_ref[pl.ds(r0, CONV_ROWS), :] = acc.reshape(CONV_ROWS, d)
        return 0

    lax.fori_loop(0, tm // CONV_ROWS, chunk, 0)


def _mixer_kernel(*refs, kind, tm, tiles_per_seq, with_router):
    if kind == "conformer":
        (x_ref, win_ref, wdw_ref, bdw_ref, cg_ref, cb_ref, wout_ref, g_ref, b_ref), refs = refs[:9], refs[9:]
    else:
        (x_ref, win_ref, wdw_ref, wout_ref, g_ref, b_ref), refs = refs[:6], refs[6:]
    if with_router:
        wr_ref, refs = refs[0], refs[1:]
        out_ref, idx_ref, rank_ref, wts_ref, cnt_ref = refs[:5]
        winb, woutb, ubuf, cbuf, wtap, carry = refs[5:]
    else:
        out_ref = refs[0]
        winb, woutb, ubuf, cbuf, wtap = refs[1:]
    i = pl.program_id(0)
    d = x_ref.shape[-1]

    @pl.when(i == 0)
    def _():
        winb[...] = win_ref[...].astype(BF16)
        woutb[...] = wout_ref[...].astype(BF16)
        for k in range(wdw_ref.shape[0]):
            wtap[k] = jnp.broadcast_to(wdw_ref[k:k + 1, :], (SUBLANES, d))
        if with_router:
            carry[...] = jnp.zeros_like(carry)

    first_tile = i % tiles_per_seq == 0
    x = x_ref[...]
    h = _mm(x.astype(BF16), winb[...])
    if kind == "conformer":
        _store_shifted(ubuf, h[:, :d] * jax.nn.sigmoid(h[:, d:]), tm, CONV_KERNEL, first_tile)
        _causal_conv(ubuf, wtap, cbuf, bdw_ref[...], tm, CONV_KERNEL)
        v = jax.nn.silu(_layer_norm(cbuf[...], cg_ref[...], cb_ref[...]))
    else:
        _store_shifted(ubuf, h[:, d:2 * d] * h[:, 2 * d:], tm, SHORT_KERNEL, first_tile)
        _causal_conv(ubuf, wtap, cbuf, jnp.zeros((1, d), F32), tm, SHORT_KERNEL)
        v = h[:, :d] * cbuf[...]
    mix = _mm(v.astype(BF16), woutb[...])
    y = _layer_norm(ALPHA * x + mix, g_ref[...], b_ref[...])
    out_ref[...] = y
    if with_router:
        _route(y, wr_ref[...], carry, idx_ref, rank_ref, wts_ref, cnt_ref)


def _mixer(kind, x, weights, ln_g, ln_b, seq_len, w_router=None):
    n, d = x.shape
    tm = 256 if kind == "conformer" else 512
    ksize = CONV_KERNEL if kind == "conformer" else SHORT_KERNEL
    assert seq_len % tm == 0 and n % seq_len == 0
    with_router = w_router is not None
    if kind == "conformer":
        w_in, w_dw, b_dw, cg, cb, w_out = weights
        ins = [x, w_in, w_dw, b_dw[None], cg[None], cb[None], w_out, ln_g[None], ln_b[None]]
    else:
        w_in, w_dw, w_out = weights
        ins = [x, w_in, w_dw, w_out, ln_g[None], ln_b[None]]
    in_specs = [pl.BlockSpec((tm, d), lambda i: (i, 0))] + [_full(a.shape) for a in ins[1:]]
    out_shape = [jax.ShapeDtypeStruct((n, d), F32)]
    out_specs = [pl.BlockSpec((tm, d), lambda i: (i, 0))]
    scratch = [pltpu.VMEM(w_in.shape, BF16), pltpu.VMEM(w_out.shape, BF16),
               pltpu.VMEM((len(_tap_residues(ksize)), tm + HALO, d), F32), pltpu.VMEM((tm, d), F32),
               pltpu.VMEM((ksize, SUBLANES, d), F32)]
    if with_router:
        wr_t = w_router.T
        ins.append(wr_t)
        in_specs.append(_full(wr_t.shape))
        rs, rspec = _router_out(n, tm)
        out_shape += rs
        out_specs += rspec
        scratch.append(pltpu.VMEM((N_EXPERTS, 1), F32))
    return pl.pallas_call(
        functools.partial(_mixer_kernel, kind=kind, tm=tm, tiles_per_seq=seq_len // tm, with_router=with_router),
        grid=(n // tm,), in_specs=in_specs, out_specs=out_specs, out_shape=out_shape,
        scratch_shapes=scratch, compiler_params=_params(("arbitrary",)), name=f"mixer_{kind}",
    )(*ins)


def _ffn_kernel(x_ref, wg_ref, wu_ref, wd_ref, g_ref, b_ref, out_ref, xb, acc):
    j = pl.program_id(1)

    @pl.when(j == 0)
    def _():
        xb[...] = x_ref[...].astype(BF16)
        acc[...] = jnp.zeros_like(acc)

    hg = _mm(xb[...], wg_ref[...].astype(BF16))
    hu = _mm(xb[...], wu_ref[...].astype(BF16))
    acc[...] += _mm((jax.nn.silu(hg) * hu).astype(BF16), wd_ref[...].astype(BF16))

    @pl.when(j == pl.num_programs(1) - 1)
    def _():
        out_ref[...] = _layer_norm(ALPHA * x_ref[...] + acc[...], g_ref[...], b_ref[...])


def _dense_ffn(x, w_gate, w_up, w_down, ln_g, ln_b):
    n, d = x.shape
    dff = w_gate.shape[1]
    tm, tf = 1024, 256
    assert n % tm == 0 and dff % tf == 0
    return pl.pallas_call(
        _ffn_kernel, grid=(n // tm, dff // tf),
        in_specs=[pl.BlockSpec((tm, d), lambda i, j: (i, 0)),
                  pl.BlockSpec((d, tf), lambda i, j: (0, j)),
                  pl.BlockSpec((d, tf), lambda i, j: (0, j)),
                  pl.BlockSpec((tf, d), lambda i, j: (j, 0)),
                  pl.BlockSpec((1, d), lambda i, j: (0, 0)),
                  pl.BlockSpec((1, d), lambda i, j: (0, 0))],
        out_specs=pl.BlockSpec((tm, d), lambda i, j: (i, 0)),
        out_shape=jax.ShapeDtypeStruct((n, d), F32),
        scratch_shapes=[pltpu.VMEM((tm, d), BF16), pltpu.VMEM((tm, d), F32)],
        compiler_params=_params(("arbitrary", "arbitrary")), name="dense_ffn",
    )(x, w_gate, w_up, w_down, ln_g[None], ln_b[None])


MOE_TM = 1024
MOE_TF = 512
MOE_TB = 256


def _row_copy(src, dst, s, t, sem):
    return pltpu.make_async_copy(src.at[pl.ds(s, 1), :], dst.at[pl.ds(t, 1), :], sem)


def _dispatch_kernel(d0_ref, d1_ref, x_ref, zeros_hbm, xs_hbm, sem):
    del zeros_hbm
    base = pl.program_id(0) * MOE_TB

    def start(t, _):
        _row_copy(x_ref, xs_hbm, t, d0_ref[base + t], sem).start()
        _row_copy(x_ref, xs_hbm, t, d1_ref[base + t], sem).start()
        return 0

    def wait(t, _):
        _row_copy(x_ref, xs_hbm, t, d0_ref[base + t], sem).wait()
        _row_copy(x_ref, xs_hbm, t, d1_ref[base + t], sem).wait()
        return 0

    lax.fori_loop(0, MOE_TB, start, 0, unroll=8)
    lax.fori_loop(0, MOE_TB, wait, 0, unroll=8)


def _dispatch(x, dest0, dest1, rows):
    n, d = x.shape
    return pl.pallas_call(
        _dispatch_kernel,
        grid_spec=pltpu.PrefetchScalarGridSpec(
            num_scalar_prefetch=2, grid=(n // MOE_TB,),
            in_specs=[pl.BlockSpec((MOE_TB, d), lambda i, d0, d1: (i, 0)), pl.BlockSpec(memory_space=pl.ANY)],
            out_specs=pl.BlockSpec(memory_space=pl.ANY),
            scratch_shapes=[pltpu.SemaphoreType.DMA(())]),
        out_shape=jax.ShapeDtypeStruct((rows, d), F32),
        input_output_aliases={3: 0},
        compiler_params=_params(("arbitrary",)), name="moe_dispatch",
    )(dest0, dest1, x, jnp.zeros((rows, d), F32))


def _expert_kernel(te_ref, tx_ref, tv_ref, x_ref, wg_ref, wu_ref, wd_ref, out_ref, xb, acc):
    del te_ref, tx_ref
    i, j = pl.program_id(0), pl.program_id(1)

    @pl.when(tv_ref[i] == 1)
    def _():
        @pl.when(j == 0)
        def _():
            xb[...] = x_ref[...].astype(BF16)
            acc[...] = jnp.zeros_like(acc)

        hg = _mm(xb[...], wg_ref[0].astype(BF16))
        hu = _mm(xb[...], wu_ref[0].astype(BF16))
        acc[...] += _mm((jax.nn.silu(hg) * hu).astype(BF16), wd_ref[0].astype(BF16))

        @pl.when(j == pl.num_programs(1) - 1)
        def _():
            out_ref[...] = acc[...]

    @pl.when((tv_ref[i] == 0) & (j == pl.num_programs(1) - 1))
    def _():
        out_ref[...] = jnp.zeros_like(out_ref)


def _experts(xs, tile_expert, tile_row, tile_valid, w_gate, w_up, w_down):
    rows, d = xs.shape
    dff = w_gate.shape[2]
    n_ff = dff // MOE_TF
    assert dff % MOE_TF == 0 and rows % MOE_TM == 0

    def jj(i, j, tv):
        return jnp.where(tv[i] == 1, j, n_ff - 1)

    return pl.pallas_call(
        _expert_kernel,
        grid_spec=pltpu.PrefetchScalarGridSpec(
            num_scalar_prefetch=3, grid=(rows // MOE_TM, n_ff),
            in_specs=[pl.BlockSpec((MOE_TM, d), lambda i, j, te, tx, tv: (tx[i], 0)),
                      pl.BlockSpec((1, d, MOE_TF), lambda i, j, te, tx, tv: (te[i], 0, jj(i, j, tv))),
                      pl.BlockSpec((1, d, MOE_TF), lambda i, j, te, tx, tv: (te[i], 0, jj(i, j, tv))),
                      pl.BlockSpec((1, MOE_TF, d), lambda i, j, te, tx, tv: (te[i], jj(i, j, tv), 0))],
            out_specs=pl.BlockSpec((MOE_TM, d), lambda i, j, te, tx, tv: (i, 0)),
            scratch_shapes=[pltpu.VMEM((MOE_TM, d), BF16), pltpu.VMEM((MOE_TM, d), F32)]),
        out_shape=jax.ShapeDtypeStruct((rows, d), F32),
        compiler_params=_params(("arbitrary", "arbitrary")), name="moe_experts",
    )(tile_expert, tile_row, tile_valid, xs, w_gate, w_up, w_down)


def _combine_kernel(d0_ref, d1_ref, x_ref, w_ref, g_ref, b_ref, ys_hbm, out_ref, ybuf, sem):
    base = pl.program_id(0) * MOE_TB

    def start(t, _):
        _row_copy(ys_hbm, ybuf.at[0], d0_ref[base + t], t, sem).start()
        _row_copy(ys_hbm, ybuf.at[1], d1_ref[base + t], t, sem).start()
        return 0

    def wait(t, _):
        _row_copy(ys_hbm, ybuf.at[0], d0_ref[base + t], t, sem).wait()
        _row_copy(ys_hbm, ybuf.at[1], d1_ref[base + t], t, sem).wait()
        return 0

    lax.fori_loop(0, MOE_TB, start, 0, unroll=8)
    lax.fori_loop(0, MOE_TB, wait, 0, unroll=8)
    w = w_ref[...]
    ffn = w[:, 0:1] * ybuf[0] + w[:, 1:2] * ybuf[1]
    out_ref[...] = _layer_norm(ALPHA * x_ref[...] + ffn, g_ref[...], b_ref[...])


def _combine(x, ys, dest0, dest1, wts, ln_g, ln_b):
    n, d = x.shape
    return pl.pallas_call(
        _combine_kernel,
        grid_spec=pltpu.PrefetchScalarGridSpec(
            num_scalar_prefetch=2, grid=(n // MOE_TB,),
            in_specs=[pl.BlockSpec((MOE_TB, d), lambda i, d0, d1: (i, 0)),
                      pl.BlockSpec((MOE_TB, TOP_K), lambda i, d0, d1: (i, 0)),
                      pl.BlockSpec((1, d), lambda i, d0, d1: (0, 0)),
                      pl.BlockSpec((1, d), lambda i, d0, d1: (0, 0)),
                      pl.BlockSpec(memory_space=pl.ANY)],
            out_specs=pl.BlockSpec((MOE_TB, d), lambda i, d0, d1: (i, 0)),
            scratch_shapes=[pltpu.VMEM((TOP_K, MOE_TB, d), F32), pltpu.SemaphoreType.DMA(())]),
        out_shape=jax.ShapeDtypeStruct((n, d), F32),
        compiler_params=_params(("arbitrary",)), name="moe_combine",
    )(dest0, dest1, x, wts, ln_g[None], ln_b[None], ys)


def _moe(x, idx, rank, wts, counts, w_gate, w_up, w_down, ln_g, ln_b):
    n, _ = x.shape
    n_tiles = (TOP_K * n) // MOE_TM + N_EXPERTS
    rows = n_tiles * MOE_TM
    cnt = counts.astype(I32)
    padded = ((cnt + MOE_TM - 1) // MOE_TM) * MOE_TM
    ends = jnp.cumsum(padded)
    offsets = ends - padded
    expert_offset = jnp.sum(jnp.where(idx[..., None] == jnp.arange(N_EXPERTS, dtype=I32), offsets, 0), axis=-1)
    dest = expert_offset + rank
    tile_start = jnp.arange(n_tiles, dtype=I32) * MOE_TM
    tile_valid = (tile_start < ends[-1]).astype(I32)
    last = jnp.maximum(ends[-1] // MOE_TM - 1, 0)
    tile_row = jnp.minimum(jnp.arange(n_tiles, dtype=I32), last)
    tile_expert = jnp.minimum(jnp.sum(tile_row[:, None] * MOE_TM >= ends[None, :], axis=1), N_EXPERTS - 1).astype(I32)
    xs = _dispatch(x, dest[0], dest[1], rows)
    ys = _experts(xs, tile_expert, tile_row, tile_valid, w_gate, w_up, w_down)
    return _combine(x, ys, dest[0], dest[1], wts.T, ln_g, ln_b)


def _nsa_proj_kernel(x_ref, win_ref, q_ref, kv_ref, gate_ref, winb, *, qd, kvd, scale):
    @pl.when(pl.program_id(0) == 0)
    def _():
        winb[...] = win_ref[...].astype(BF16)

    h = _mm(x_ref[...].astype(BF16), winb[...])
    q_ref[...] = h[:, :qd] * scale
    kv_ref[...] = h[:, qd:qd + kvd]
    gate_ref[...] = jax.nn.sigmoid(h[:, qd + kvd:])


def _nsa_proj(x, w_in, qd, kvd):
    n, d = x.shape
    cols = w_in.shape[1]
    gd = cols - qd - kvd
    tm = 512
    dh = qd // NSA_HEADS
    return pl.pallas_call(
        functools.partial(_nsa_proj_kernel, qd=qd, kvd=kvd, scale=dh ** -0.5),
        grid=(n // tm,),
        in_specs=[pl.BlockSpec((tm, d), lambda i: (i, 0)), _full(w_in.shape)],
        out_specs=[pl.BlockSpec((tm, qd), lambda i: (i, 0)), pl.BlockSpec((tm, kvd), lambda i: (i, 0)),
                   pl.BlockSpec((tm, gd), lambda i: (i, 0))],
        out_shape=[jax.ShapeDtypeStruct((n, qd), F32), jax.ShapeDtypeStruct((n, kvd), F32),
                   jax.ShapeDtypeStruct((n, gd), F32)],
        scratch_shapes=[pltpu.VMEM(w_in.shape, BF16)],
        compiler_params=_params(("arbitrary",)), name="nsa_proj",
    )(x, w_in)


def _compress_kernel(ck_ref, cv_ref, pek_ref, pev_ref, wk1_ref, wk1c_ref, wk2_ref, wv1_ref, wv1c_ref, wv2_ref,
                     ko_ref, vo_ref):
    def one(c_ref, pe_ref, w1_ref, w1c_ref, w2_ref, o_ref):
        hid = w2_ref.shape[0]
        a = _mm(c_ref[0].astype(BF16), w1c_ref[...].astype(BF16))
        bias = _mm(pe_ref[...].astype(BF16), w1_ref[...].astype(BF16))[0:1, :]
        pre = a[:, :hid] + pltpu.roll(a[:, hid:], a.shape[0] - 1, 0) + bias
        o_ref[0] = _mm(jax.nn.gelu(pre).astype(BF16), w2_ref[...].astype(BF16))

    one(ck_ref, pek_ref, wk1_ref, wk1c_ref, wk2_ref, ko_ref)
    one(cv_ref, pev_ref, wv1_ref, wv1c_ref, wv2_ref, vo_ref)


def _compress(ck, cv, pe_k, pe_v, wk1, wk2, wv1, wv2):
    bh, chunks, cw = ck.shape
    dh = wk2.shape[1]

    def prep(pe, w1):
        pe_rows = jnp.zeros((SUBLANES, pe.size), F32).at[0].set(pe.reshape(-1))
        return pe_rows, jnp.concatenate([w1[:cw], w1[cw:]], axis=1)

    pek, wk1c = prep(pe_k, wk1)
    pev, wv1c = prep(pe_v, wv1)
    ins = [ck, cv, pek, pev, wk1, wk1c, wk2, wv1, wv1c, wv2]
    blk = pl.BlockSpec((1, chunks, cw), lambda i: (i, 0, 0))
    oblk = pl.BlockSpec((1, chunks, dh), lambda i: (i, 0, 0))
    return pl.pallas_call(
        _compress_kernel, grid=(bh,),
        in_specs=[blk, blk] + [_full(a.shape) for a in ins[2:]],
        out_specs=[oblk, oblk],
        out_shape=[jax.ShapeDtypeStruct((bh, chunks, dh), F32)] * 2,
        compiler_params=_params(("arbitrary",)), name="nsa_compress",
    )(*ins)


NSA_TQ = 256
NSA_TK = 256


def _masked_softmax(s, mask):
    s = jnp.where(mask, s, NEG_INF)
    m = jnp.max(s, -1, keepdims=True)
    e = jnp.exp(s - m) * mask.astype(F32)
    return e / jnp.maximum(jnp.sum(e, -1, keepdims=True), 1e-30)


def _attn_kernel(q_ref, gate_ref, kc_ref, vc_ref, ks_ref, vs_ref, kw_ref, vw_ref, o_ref, s_scr, m_scr, l_scr, acc_scr):
    g, tq, dh = q_ref.shape[1:]
    m_rows = g * tq
    n_cmp_pad = kc_ref.shape[1]
    n_slc = ks_ref.shape[1] // SLC_BLOCK
    qi = pl.program_id(1)
    t0 = qi * tq
    qb = q_ref[0].reshape(m_rows, dh).astype(BF16)
    t_col = t0 + lax.broadcasted_iota(I32, (tq, 1), 0)
    t_rows = jnp.concatenate([t_col] * g, axis=0)

    c_idx = lax.broadcasted_iota(I32, (tq, n_cmp_pad), 1)
    mask_c = (c_idx * CMP_STRIDE + CMP_BLOCK - 1 <= t_col) & (c_idx < n_cmp_pad - 1)
    s_c = _nt(qb, kc_ref[0].astype(BF16)).reshape(g, tq, n_cmp_pad)
    p_c = _masked_softmax(s_c, jnp.broadcast_to(mask_c[None], s_c.shape))
    o_cmp = _mm(p_c.reshape(m_rows, n_cmp_pad).astype(BF16), vc_ref[0].astype(BF16))

    p_sum = jnp.sum(p_c, axis=0)
    n_i = lax.broadcasted_iota(I32, (n_slc, n_cmp_pad), 0) * SLC_BLOCK
    c_s = lax.broadcasted_iota(I32, (n_slc, n_cmp_pad), 1) * CMP_STRIDE
    overlap_t = ((c_s < n_i + SLC_BLOCK) & (c_s + CMP_BLOCK > n_i)).astype(F32)
    imp_t = _nt(overlap_t, p_sum, precision=lax.Precision.HIGHEST)
    blk = lax.broadcasted_iota(I32, (n_slc, tq), 0)
    t_lane = t0 + lax.broadcasted_iota(I32, (n_slc, tq), 1)
    d_cur = t_lane // SLC_BLOCK - blk
    forced = (blk == 0) | ((d_cur >= 0) & (d_cur < SLC_FORCED_LOCAL))
    score = jnp.where(forced, FORCE_BONUS, jnp.where(blk * SLC_BLOCK <= t_lane, imp_t, NEG_INF))
    rank = jnp.zeros((n_slc, tq), I32)
    for m in range(n_slc):
        sm = score[m:m + 1, :]
        rank = rank + ((sm > score) | ((sm == score) & (blk > m))).astype(I32)
    pen_t = jnp.where(rank < min(SLC_TOP_N, n_slc), 0.0, SEL_PENALTY).astype(BF16)
    eye = (lax.broadcasted_iota(I32, (tq, tq), 0) == lax.broadcasted_iota(I32, (tq, tq), 1)).astype(BF16)
    pen = _nt(eye, pen_t).astype(BF16)
    pen_rows = jnp.concatenate([pen] * g, axis=0)

    assert tq == NSA_TK and NSA_TK % LANES == 0
    lane_blocks = NSA_TK // LANES
    m_scr[...] = jnp.full(m_scr.shape, NEG_INF, F32)

    def score_tile(kt, diagonal):
        k0 = pl.multiple_of(kt * NSA_TK, NSA_TK)
        kb = ks_ref[0, pl.ds(k0, NSA_TK), :].astype(BF16)
        j_pos = k0 + lax.broadcasted_iota(I32, (n_slc, NSA_TK), 1)
        expand = (j_pos // SLC_BLOCK == lax.broadcasted_iota(I32, (n_slc, NSA_TK), 0)).astype(BF16)
        s = _nt(qb, kb) + _mm(pen_rows, expand)
        if diagonal:
            key_pos = k0 + lax.broadcasted_iota(I32, (m_rows, NSA_TK), 1)
            s = jnp.where(key_pos <= t_rows, s, NEG_INF)
        s_scr[kt] = s
        mx = m_scr[...]
        for c in range(lane_blocks):
            mx = jnp.maximum(mx, s[:, c * LANES:(c + 1) * LANES])
        m_scr[...] = mx

    def score_step(kt, _):
        score_tile(kt, False)
        return 0

    lax.fori_loop(0, qi, score_step, 0)
    score_tile(qi, True)
    m_scr[...] = jnp.broadcast_to(jnp.max(m_scr[...], -1, keepdims=True), m_scr.shape)
    l_scr[...] = jnp.zeros_like(l_scr)
    acc_scr[...] = jnp.zeros_like(acc_scr)

    def value_step(kt, _):
        k0 = pl.multiple_of(kt * NSA_TK, NSA_TK)
        vb = vs_ref[0, pl.ds(k0, NSA_TK), :].astype(BF16)
        s = s_scr[kt]
        row_max = m_scr[...]
        lsum = l_scr[...]
        parts = []
        for c in range(lane_blocks):
            p = jnp.exp(s[:, c * LANES:(c + 1) * LANES] - row_max)
            lsum = lsum + p
            parts.append(p.astype(BF16))
        l_scr[...] = lsum
        acc_scr[...] += _mm(jnp.concatenate(parts, axis=1), vb)
        return 0

    lax.fori_loop(0, qi + 1, value_step, 0)
    o_sel = acc_scr[...] / jnp.sum(l_scr[...], -1, keepdims=True)

    span = WINDOW + tq
    kwin = kw_ref[0, pl.ds(pl.multiple_of(t0, tq), span), :].astype(BF16)
    vwin = vw_ref[0, pl.ds(pl.multiple_of(t0, tq), span), :].astype(BF16)
    kpos = t0 - WINDOW + lax.broadcasted_iota(I32, (m_rows, span), 1)
    diff = t_rows - kpos
    p_w = _masked_softmax(_nt(qb, kwin), (diff >= 0) & (diff < WINDOW) & (kpos >= 0))
    o_win = _mm(p_w.astype(BF16), vwin)

    gate = gate_ref[0].reshape(m_rows, 3)
    o = gate[:, 0:1] * o_cmp + gate[:, 1:2] * o_sel + gate[:, 2:3] * o_win
    o_ref[0] = o.reshape(g, tq, dh)


def _attention(q5, gates, k_cmp, v_cmp, ks, vs, kw_pad, vw_pad):
    bh, g, t, dh = q5.shape
    tq = NSA_TQ
    n_cmp_pad = k_cmp.shape[1]
    m_rows = g * tq
    qblk = pl.BlockSpec((1, g, tq, dh), lambda b, i: (b, 0, i, 0))
    cblk = pl.BlockSpec((1, n_cmp_pad, dh), lambda b, i: (b, 0, 0))
    kblk = pl.BlockSpec((1, t, dh), lambda b, i: (b, 0, 0))
    wblk = pl.BlockSpec((1, t + WINDOW, dh), lambda b, i: (b, 0, 0))
    return pl.pallas_call(
        _attn_kernel, grid=(bh, t // tq),
        in_specs=[qblk, pl.BlockSpec((1, g, tq, 3), lambda b, i: (b, 0, i, 0)), cblk, cblk, kblk, kblk, wblk, wblk],
        out_specs=qblk,
        out_shape=jax.ShapeDtypeStruct((bh, g, t, dh), F32),
        scratch_shapes=[pltpu.VMEM((t // NSA_TK, m_rows, NSA_TK), F32), pltpu.VMEM((m_rows, LANES), F32),
                        pltpu.VMEM((m_rows, LANES), F32), pltpu.VMEM((m_rows, dh), F32)],
        compiler_params=_params(("arbitrary", "arbitrary")), name="nsa_attention",
    )(q5, gates, k_cmp, v_cmp, ks, vs, kw_pad, vw_pad)


def _outproj_kernel(o_ref, x_ref, w_ref, g_ref, b_ref, out_ref, wb):
    @pl.when(pl.program_id(0) == 0)
    def _():
        wb[...] = w_ref[...].astype(BF16)

    mix = _mm(o_ref[...].astype(BF16), wb[...])
    out_ref[...] = _layer_norm(ALPHA * x_ref[...] + mix, g_ref[...], b_ref[...])


def _outproj(o, x, w_out, ln_g, ln_b):
    n, d = x.shape
    tm = 512
    blk = pl.BlockSpec((tm, d), lambda i: (i, 0))
    return pl.pallas_call(
        _outproj_kernel, grid=(n // tm,),
        in_specs=[blk, blk, _full(w_out.shape), _full((1, d)), _full((1, d))],
        out_specs=blk, out_shape=jax.ShapeDtypeStruct((n, d), F32),
        scratch_shapes=[pltpu.VMEM(w_out.shape, BF16)],
        compiler_params=_params(("arbitrary",)), name="nsa_outproj",
    )(o, x, w_out, ln_g[None], ln_b[None])


def _nsa_mixer(x, batch, seq, w_in, pe_k, pe_v, wk1, wk2, wv1, wv2, w_out, ln_g, ln_b):
    n, d = x.shape
    h, hk, g = NSA_HEADS, NSA_KV_HEADS, NSA_GROUP
    dh = d // h
    qd, kd = h * dh, hk * dh
    q, kv, gates = _nsa_proj(x, w_in, qd, 6 * kd)
    q5 = q.reshape(batch, seq, hk, g, dh).transpose(0, 2, 3, 1, 4).reshape(batch * hk, g, seq, dh)
    gates = gates.reshape(batch, seq, hk, g, 3).transpose(0, 2, 3, 1, 4).reshape(batch * hk, g, seq, 3)
    kv = kv.reshape(batch, seq, 6, hk, dh).transpose(2, 0, 3, 1, 4).reshape(6, batch * hk, seq, dh)
    kc, vc, ks, vs, kw, vw = (kv[i] for i in range(6))
    chunks = seq // CMP_STRIDE
    k_cmp, v_cmp = _compress(kc.reshape(batch * hk, chunks, CMP_STRIDE * dh), vc.reshape(batch * hk, chunks, CMP_STRIDE * dh),
                             pe_k, pe_v, wk1, wk2, wv1, wv2)
    pad = ((0, 0), (WINDOW, 0), (0, 0))
    o5 = _attention(q5, gates, k_cmp, v_cmp, ks, vs, jnp.pad(kw, pad), jnp.pad(vw, pad))
    o = o5.reshape(batch, hk, g, seq, dh).transpose(0, 3, 1, 2, 4).reshape(n, d)
    return _outproj(o, x, w_out, ln_g, ln_b)


def kernel(x, ln_mix_g, ln_mix_b, ln_ffn_g, ln_ffn_b, conf_w_in, conf_w_dw, conf_b_dw, conf_ln_g, conf_ln_b, conf_w_out, sc_w_in, sc_w_conv, sc_w_out, nsa_w_in, nsa_pe_k, nsa_pe_v, nsa_wk1, nsa_wk2, nsa_wv1, nsa_wv2, nsa_w_out, ffn_w_gate, ffn_w_up, ffn_w_down, moe_w_router, moe_w_gate, moe_w_up, moe_w_down):
    batch, seq, d = x.shape
    h = x.reshape(batch * seq, d)
    for i in range(DEPTH):
        kind, j = i % N_MIXERS, i // N_MIXERS
        f = i // 2
        w_router = moe_w_router[f] if i % 2 == 1 else None
        if kind == 0:
            res = _mixer("conformer", h, (conf_w_in[j], conf_w_dw[j], conf_b_dw[j], conf_ln_g[j], conf_ln_b[j], conf_w_out[j]),
                         ln_mix_g[i], ln_mix_b[i], seq, w_router)
        elif kind == 1:
            res = _mixer("shortconv", h, (sc_w_in[j], sc_w_conv[j], sc_w_out[j]), ln_mix_g[i], ln_mix_b[i], seq, w_router)
        else:
            assert w_router is None
            res = [_nsa_mixer(h, batch, seq, nsa_w_in[j], nsa_pe_k[j], nsa_pe_v[j], nsa_wk1[j], nsa_wk2[j],
                              nsa_wv1[j], nsa_wv2[j], nsa_w_out[j], ln_mix_g[i], ln_mix_b[i])]
        h = res[0]
        if i % 2 == 0:
            h = _dense_ffn(h, ffn_w_gate[f], ffn_w_up[f], ffn_w_down[f], ln_ffn_g[i], ln_ffn_b[i])
        else:
            _, idx, rank, wts, cnt = res
            h = _moe(h, idx, rank, wts, cnt[:, 0], moe_w_gate[f], moe_w_up[f], moe_w_down[f], ln_ffn_g[i], ln_ffn_b[i])
    return h.reshape(batch, seq, d)
```
